```python
import math
import jax, jax.numpy as jnp
from jax import lax
import numpy as np

D_MODEL = 1024
BATCH = 2
SEQ = 8192
DEPTH = 4
DEC_BATCH = 128
DEC_SEQ = 8
PAST_LEN = 8192
PAGE_SIZE = 128

MIX = D_MODEL
GLA_HEADS = 4
GLA_DK = 32
GLA_DV = 64
GLA_GATE_RANK = 16
GLA_GATE_TAU = 16.0
GLA_CHUNK = 16
DIFF_HEADS = 4
DIFF_KV_HEADS = 2
DIFF_D = 32
MLA_HEADS = 8
MLA_Q_RANK = 768
MLA_KV_RANK = 256
MLA_NOPE = 64
MLA_ROPE = 32
MLA_V = 64
ROPE_BASE = 10000.0
D_FF = 2816
CONV_W = 3
Q_BLOCK = 128
EPS = 1e-6

_SPLITS = (GLA_HEADS * GLA_DK, GLA_HEADS * GLA_DK, GLA_HEADS * GLA_DV, GLA_HEADS * GLA_DV, GLA_GATE_RANK,
           DIFF_HEADS * 2 * DIFF_D, DIFF_KV_HEADS * 2 * DIFF_D, DIFF_KV_HEADS * 2 * DIFF_D,
           MLA_Q_RANK, MLA_KV_RANK, MLA_ROPE)
P_IN = sum(_SPLITS)
SPLIT_IDX = tuple(np.cumsum(_SPLITS)[:-1].tolist())

kernel_name = "hymba_gla_diff_mla_convffn_step"


def rmsnorm(x, w):
    xf = x.astype(jnp.float32)
    y = xf * lax.rsqrt(jnp.mean(xf * xf, axis=-1, keepdims=True) + EPS)
    return (y * w).astype(x.dtype)


def rope(x, pos):
    half = x.shape[-1] // 2
    inv = ROPE_BASE ** (-jnp.arange(half, dtype=jnp.float32) / half)
    ang = pos.astype(jnp.float32)[:, None] * inv[None, :]
    shape = (1, pos.shape[0]) + (1,) * (x.ndim - 3) + (half,)
    cos = jnp.cos(ang).reshape(shape)
    sin = jnp.sin(ang).reshape(shape)
    x1 = x[..., :half].astype(jnp.float32)
    x2 = x[..., half:].astype(jnp.float32)
    return jnp.concatenate([x1 * cos - x2 * sin, x2 * cos + x1 * sin], axis=-1).astype(x.dtype)


def gla_chunked(q, k, v, log_a, s0):
    B, L, H, DK = q.shape
    DV = v.shape[-1]
    C = math.gcd(L, GLA_CHUNK)
    n = L // C
    f32 = jnp.float32
    qc = q.reshape(B, n, C, H, DK).astype(f32)
    kc = k.reshape(B, n, C, H, DK).astype(f32)
    vc = v.reshape(B, n, C, H, DV).astype(f32)
    b = jnp.cumsum(log_a.reshape(B, n, C, H, DK).astype(f32), axis=2)
    causal = jnp.tril(jnp.ones((C, C), bool))[:, :, None, None]
    decay = jnp.exp(jnp.where(causal, b[:, :, :, None] - b[:, :, None, :], -jnp.inf))
    attn = jnp.einsum('bnthd,bnshd,bntshd->bnhts', qc, kc, decay)
    o_intra = jnp.einsum('bnhts,bnshv->bnthv', attn, vc)
    b_last = b[:, :, -1]
    ds = jnp.einsum('bnshd,bnshv->bnhdv', kc * jnp.exp(b_last[:, :, None] - b), vc)

    def step(s, inp):
        g, d = inp
        return jnp.exp(g)[..., None] * s + d, s

    s_final, s_starts = lax.scan(step, s0.astype(f32), (jnp.moveaxis(b_last, 1, 0), jnp.moveaxis(ds, 1, 0)))
    o_inter = jnp.einsum('bnthd,nbhdv->bnthv', qc * jnp.exp(b), s_starts)
    o = (o_intra + o_inter).reshape(B, L, H, DV)
    return o.astype(v.dtype), s_final.astype(s0.dtype)


def diff_attn_core(q, k, v, q_pos, k_pos, lam):
    B, Lq = q.shape[:2]
    G = DIFF_HEADS // DIFF_KV_HEADS
    qr = q.reshape(B, Lq, DIFF_KV_HEADS, G, 2, DIFF_D)
    kr = k.reshape(B, k.shape[1], DIFF_KV_HEADS, 2, DIFF_D)
    s = jnp.einsum('bqkgmd,bskmd->bmkgqs', qr, kr).astype(jnp.float32) * (DIFF_D ** -0.5)
    mask = k_pos[None, :] <= q_pos[:, None]
    p = jax.nn.softmax(jnp.where(mask, s, -jnp.inf), axis=-1)
    a = p[:, 0] - lam * p[:, 1]
    o = jnp.einsum('bkgqs,bskd->bqkgd', a.astype(v.dtype), v)
    return o.reshape(B, Lq, DIFF_HEADS, 2 * DIFF_D)


def mla_core_full(q_nope, q_rope, k_nope, k_rope, v, q_pos, k_pos):
    s = (jnp.einsum('bqhd,bshd->bhqs', q_nope, k_nope)
         + jnp.einsum('bqhr,bsr->bhqs', q_rope, k_rope)).astype(jnp.float32) * ((MLA_NOPE + MLA_ROPE) ** -0.5)
    mask = k_pos[None, :] <= q_pos[:, None]
    p = jax.nn.softmax(jnp.where(mask, s, -jnp.inf), axis=-1)
    return jnp.einsum('bhqs,bshd->bqhd', p.astype(v.dtype), v)


def mla_core_absorbed(q_nope, q_rope, latent, k_rope, w_uk, w_uv, q_pos, k_pos):
    q_lat = jnp.einsum('bqhd,chd->bqhc', q_nope, w_uk)
    s = (jnp.einsum('bqhc,bsc->bhqs', q_lat, latent)
         + jnp.einsum('bqhr,bsr->bhqs', q_rope, k_rope)).astype(jnp.float32) * ((MLA_NOPE + MLA_ROPE) ** -0.5)
    mask = k_pos[None, :] <= q_pos[:, None]
    p = jax.nn.softmax(jnp.where(mask, s, -jnp.inf), axis=-1)
    o_lat = jnp.einsum('bhqs,bsc->bqhc', p.astype(latent.dtype), latent)
    return jnp.einsum('bqhc,chd->bqhd', o_lat, w_uv)


def sweep_query_blocks(core, qs, pos):
    B, L = qs[0].shape[:2]
    nb = L // Q_BLOCK
    qb = tuple(jnp.moveaxis(q.reshape((B, nb, Q_BLOCK) + q.shape[2:]), 1, 0) for q in qs)
    pb = pos.reshape(nb, Q_BLOCK)
    out = lax.map(lambda a: core(*a[0], a[1]), (qb, pb))
    return jnp.moveaxis(out, 0, 1).reshape((B, L) + out.shape[3:])


def token_mixers(hn, pos, lp, lam, lam_init, past):
    B, L, _ = hn.shape
    (g_q, g_k, g_v, g_gate, g_low, d_q, d_k, d_v, c_q, c_kv, k_r) = jnp.split(hn @ lp['w_in'], SPLIT_IDX, axis=-1)
    gla_q = g_q.reshape(B, L, GLA_HEADS, GLA_DK) * (GLA_DK ** -0.5)
    gla_k = g_k.reshape(B, L, GLA_HEADS, GLA_DK)
    gla_v = g_v.reshape(B, L, GLA_HEADS, GLA_DV)
    log_a = jax.nn.log_sigmoid((g_low @ lp['gla_w_gate'] + lp['gla_b_gate']).astype(jnp.float32))
    log_a = log_a.reshape(B, L, GLA_HEADS, GLA_DK) / GLA_GATE_TAU
    diff_q = d_q.reshape(B, L, DIFF_HEADS, 2 * DIFF_D)
    diff_k = d_k.reshape(B, L, DIFF_KV_HEADS, 2 * DIFF_D)
    diff_v = d_v.reshape(B, L, DIFF_KV_HEADS, 2 * DIFF_D)
    q_full = (rmsnorm(c_q, lp['mla_q_norm']) @ lp['mla_w_uq']).reshape(B, L, MLA_HEADS, MLA_NOPE + MLA_ROPE)
    q_nope = q_full[..., :MLA_NOPE]
    q_rope = rope(q_full[..., MLA_NOPE:], pos)
    latent = rmsnorm(c_kv, lp['mla_kv_norm'])
    k_rope = rope(k_r, pos)
    if past is None:
        s0 = jnp.zeros((B, GLA_HEADS, GLA_DK, GLA_DV), hn.dtype)
        o_gla, s_new = gla_chunked(gla_q, gla_k, gla_v, log_a, s0)
        o_diff = sweep_query_blocks(lambda q, qp: diff_attn_core(q, diff_k, diff_v, qp, pos, lam), (diff_q,), pos)
        k_nope = jnp.einsum('blc,chd->blhd', latent, lp['mla_w_uk'])
        v_mla = jnp.einsum('blc,chd->blhd', latent, lp['mla_w_uv'])
        o_mla = sweep_query_blocks(
            lambda qn, qr, qp: mla_core_full(qn, qr, k_nope, k_rope, v_mla, qp, pos), (q_nope, q_rope), pos)
    else:
        o_gla, s_new = gla_chunked(gla_q, gla_k, gla_v, log_a, past['gla'])
        k_pos = jnp.arange(past['diff_k'].shape[1] + L)
        all_dk = jnp.concatenate([past['diff_k'].astype(diff_k.dtype), diff_k], axis=1)
        all_dv = jnp.concatenate([past['diff_v'].astype(diff_v.dtype), diff_v], axis=1)
        o_diff = diff_attn_core(diff_q, all_dk, all_dv, pos, k_pos, lam)
        all_lat = jnp.concatenate([past['mla_latent'].astype(latent.dtype), latent], axis=1)
        all_kr = jnp.concatenate([past['mla_rope'].astype(k_rope.dtype), k_rope], axis=1)
        o_mla = mla_core_absorbed(q_nope, q_rope, all_lat, all_kr, lp['mla_w_uk'], lp['mla_w_uv'], pos, k_pos)
    o_gla = rmsnorm(o_gla, lp['gla_norm']) * jax.nn.silu(g_gate.reshape(B, L, GLA_HEADS, GLA_DV))
    o_diff = rmsnorm(o_diff, lp['diff_norm']) * (1.0 - lam_init)
    merged = jnp.concatenate([o_gla.reshape(B, L, -1), o_diff.reshape(B, L, -1), o_mla.reshape(B, L, -1)], axis=-1)
    return merged @ lp['w_o'], (diff_k, diff_v, latent, k_rope, s_new)


def conv_ffn(hn, lp, prev):
    L = hn.shape[1]
    a, b = jnp.split(hn @ lp['ffn_w_up'], 2, axis=-1)
    ext = jnp.concatenate([prev.astype(a.dtype), a], axis=1)
    c = lp['ffn_conv_b'] + ext[:, 0:L] * lp['ffn_conv_w'][0]
    for j in range(1, CONV_W):
        c = c + ext[:, j:j + L] * lp['ffn_conv_w'][j]
    y = (jax.nn.gelu(c, approximate=True) * b) @ lp['ffn_w_down']
    return y, ext[:, L:]


def layer(x, pos, lp, lam, lam_init, past, conv_prev):
    m, new = token_mixers(rmsnorm(x, lp['norm_mix_pre']), pos, lp, lam, lam_init, past)
    x = x + rmsnorm(m, lp['norm_mix_post'])
    f, conv_new = conv_ffn(rmsnorm(x, lp['norm_ffn_pre']), lp, conv_prev)
    x = x + rmsnorm(f, lp['norm_ffn_post'])
    return x, new + (conv_new,)


def setup_inputs(seed: int = 0) -> dict:
    key = jax.random.key(seed)
    ks = jax.random.split(key, 32)
    f32 = jnp.float32
    n_pages = PAST_LEN // PAGE_SIZE
    used = DEC_BATCH * n_pages
    n_phys = used + max(1, used // 4)

    def nrm(k, shape, scale=1.0):
        return jax.random.normal(k, shape, f32) * scale

    def gain(k, shape):
        return 1.0 + 0.05 * jax.random.normal(k, shape, f32)

    page_table = jax.random.permutation(ks[0], n_phys)[:used].reshape(DEC_BATCH, n_pages).astype(jnp.int32)
    return {
        'x_prompt': nrm(ks[1], (BATCH, SEQ, D_MODEL)),
        'x_sample': nrm(ks[2], (DEC_BATCH, DEC_SEQ, D_MODEL)),
        'cache_diff_k': nrm(ks[3], (DEPTH, n_phys, PAGE_SIZE, DIFF_KV_HEADS, 2 * DIFF_D)),
        'cache_diff_v': nrm(ks[4], (DEPTH, n_phys, PAGE_SIZE, DIFF_KV_HEADS, 2 * DIFF_D)),
        'cache_mla_latent': nrm(ks[5], (DEPTH, n_phys, PAGE_SIZE, MLA_KV_RANK)),
        'cache_mla_rope': nrm(ks[6], (DEPTH, n_phys, PAGE_SIZE, MLA_ROPE)),
        'state_gla': nrm(ks[7], (DEPTH, DEC_BATCH, GLA_HEADS, GLA_DK, GLA_DV)),
        'state_ffn_conv': nrm(ks[8], (DEPTH, DEC_BATCH, CONV_W - 1, D_FF)),
        'page_table': page_table,
        'norm_mix_pre': gain(ks[9], (DEPTH, D_MODEL)),
        'norm_mix_post': gain(ks[10], (DEPTH, D_MODEL)),
        'norm_ffn_pre': gain(ks[11], (DEPTH, D_MODEL)),
        'norm_ffn_post': gain(ks[12], (DEPTH, D_MODEL)),
        'w_in': nrm(ks[13], (DEPTH, D_MODEL, P_IN), D_MODEL ** -0.5),
        'gla_w_gate': nrm(ks[14], (DEPTH, GLA_GATE_RANK, GLA_HEADS * GLA_DK), GLA_GATE_RANK ** -0.5),
        'gla_b_gate': nrm(ks[15], (DEPTH, GLA_HEADS * GLA_DK), 0.1),
        'gla_norm': gain(ks[16], (DEPTH, GLA_DV)),
        'diff_lambda': nrm(ks[17], (DEPTH, 4, DIFF_D), 0.1),
        'diff_norm': gain(ks[18], (DEPTH, 2 * DIFF_D)),
        'mla_q_norm': gain(ks[19], (DEPTH, MLA_Q_RANK)),
        'mla_w_uq': nrm(ks[20], (DEPTH, MLA_Q_RANK, MLA_HEADS * (MLA_NOPE + MLA_ROPE)), MLA_Q_RANK ** -0.5),
        'mla_kv_norm': gain(ks[21], (DEPTH, MLA_KV_RANK)),
        'mla_w_uk': nrm(ks[22], (DEPTH, MLA_KV_RANK, MLA_HEADS, MLA_NOPE), MLA_KV_RANK ** -0.5),
        'mla_w_uv': nrm(ks[23], (DEPTH, MLA_KV_RANK, MLA_HEADS, MLA_V), MLA_KV_RANK ** -0.5),
        'w_o': nrm(ks[24], (DEPTH, MIX, D_MODEL), MIX ** -0.5),
        'ffn_w_up': nrm(ks[25], (DEPTH, D_MODEL, 2 * D_FF), D_MODEL ** -0.5),
        'ffn_conv_w': nrm(ks[26], (DEPTH, CONV_W, D_FF), CONV_W ** -0.5),
        'ffn_conv_b': nrm(ks[27], (DEPTH, D_FF), 0.01),
        'ffn_w_down': nrm(ks[28], (DEPTH, D_FF, D_MODEL), D_FF ** -0.5),
    }


def reference(x_prompt, x_sample, cache_diff_k, cache_diff_v, cache_mla_latent, cache_mla_rope, state_gla,
              state_ffn_conv, page_table, norm_mix_pre, norm_mix_post, norm_ffn_pre, norm_ffn_post, w_in,
              gla_w_gate, gla_b_gate, gla_norm, diff_lambda, diff_norm, mla_q_norm, mla_w_uq, mla_kv_norm,
              mla_w_uk, mla_w_uv, w_o, ffn_w_up, ffn_conv_w, ffn_conv_b, ffn_w_down):
    past_len = page_table.shape[1] * PAGE_SIZE
    nb_s = x_sample.shape[0]
    pos_p = jnp.arange(x_prompt.shape[1])
    pos_s = past_len + jnp.arange(x_sample.shape[1])

    def gather(cache, l):
        g = cache[l, page_table]
        return g.reshape((nb_s, past_len) + cache.shape[3:])

    hp, hs = x_prompt, x_sample
    pn = [[] for _ in range(6)]
    sn = [[] for _ in range(6)]
    for l in range(DEPTH):
        lp = {
            'norm_mix_pre': norm_mix_pre[l], 'norm_mix_post': norm_mix_post[l],
            'norm_ffn_pre': norm_ffn_pre[l], 'norm_ffn_post': norm_ffn_post[l],
            'w_in': w_in[l], 'gla_w_gate': gla_w_gate[l], 'gla_b_gate': gla_b_gate[l], 'gla_norm': gla_norm[l],
            'diff_norm': diff_norm[l], 'mla_q_norm': mla_q_norm[l], 'mla_w_uq': mla_w_uq[l],
            'mla_kv_norm': mla_kv_norm[l], 'mla_w_uk': mla_w_uk[l], 'mla_w_uv': mla_w_uv[l], 'w_o': w_o[l],
            'ffn_w_up': ffn_w_up[l], 'ffn_conv_w': ffn_conv_w[l], 'ffn_conv_b': ffn_conv_b[l],
            'ffn_w_down': ffn_w_down[l],
        }
        lam_init = 0.8 - 0.6 * math.exp(-0.3 * l)
        dl = diff_lambda[l].astype(jnp.float32)
        lam = jnp.exp(jnp.sum(dl[0] * dl[1])) - jnp.exp(jnp.sum(dl[2] * dl[3])) + lam_init
        conv0 = jnp.zeros((x_prompt.shape[0], CONV_W - 1, D_FF), x_prompt.dtype)
        hp, newp = layer(hp, pos_p, lp, lam, lam_init, None, conv0)
        past = {
            'gla': state_gla[l],
            'diff_k': gather(cache_diff_k, l), 'diff_v': gather(cache_diff_v, l),
            'mla_latent': gather(cache_mla_latent, l), 'mla_rope': gather(cache_mla_rope, l),
        }
        hs, news = layer(hs, pos_s, lp, lam, lam_init, past, state_ffn_conv[l])
        for i in range(6):
            pn[i].append(newp[i])
            sn[i].append(news[i])
    return (hp, hs,
            jnp.stack(pn[0]), jnp.stack(pn[1]), jnp.stack(pn[2]), jnp.stack(pn[3]), jnp.stack(pn[4]), jnp.stack(pn[5]),
            jnp.stack(sn[0]), jnp.stack(sn[1]), jnp.stack(sn[2]), jnp.stack(sn[3]), jnp.stack(sn[4]), jnp.stack(sn[5]))
```

```python
import functools
import math

import jax
import jax.numpy as jnp
import numpy as np
from jax import lax
from jax.experimental import pallas as pl
from jax.experimental.pallas import tpu as pltpu

F32 = jnp.float32
BF16 = jnp.bfloat16

D_MODEL = 1024
GLA_HEADS, GLA_DK, GLA_DV = 4, 32, 64
GLA_GATE_RANK, GLA_GATE_TAU, GLA_CHUNK = 16, 16.0, 16
DIFF_HEADS, DIFF_KV_HEADS, DIFF_D = 4, 2, 32
MLA_HEADS, MLA_Q_RANK, MLA_KV_RANK, MLA_NOPE, MLA_ROPE, MLA_V = 8, 768, 256, 64, 32, 64
ROPE_BASE = 10000.0
D_FF = 2816
EPS = 1e-6
PAGE = 128

LANES = 128
VMEM_LIMIT = 56 * 1024 * 1024
NEG_INF = float("-inf")

GLA_BLOCK = 128
FFN_TF = 256
PAGES_PER_STEP = 16


def _cparams(sem):
    return pltpu.CompilerParams(dimension_semantics=sem, vmem_limit_bytes=VMEM_LIMIT)


def _rms(x, g):
    return x * lax.rsqrt(jnp.mean(x * x, axis=-1, keepdims=True) + EPS) * g


def _dot(a, b):
    return jnp.dot(a, b, preferred_element_type=F32)


def _dot_nt(a, b):
    return lax.dot_general(a, b, (((1,), (1,)), ((), ())), preferred_element_type=F32)


def _div(x, n):
    assert n & (n - 1) == 0
    return lax.shift_right_logical(x, int(n).bit_length() - 1)


def _mod(x, n):
    assert n & (n - 1) == 0
    return x & (n - 1)


def _split3(x):
    x1 = x.astype(BF16)
    r = x - x1.astype(F32)
    x2 = r.astype(BF16)
    x3 = (r - x2.astype(F32)).astype(BF16)
    return x1, x2, x3


def _group_mean(sq, e_avg):
    hi = sq.astype(BF16)
    lo = (sq - hi.astype(F32)).astype(BF16)
    return _dot(hi, e_avg) + _dot(lo, e_avg)


def _in_proj_kernel(x_ref, gpre_ref, wgla_ref, wgate_ref, bgate_ref, wdiff_ref, wmla_ref, gq_ref, wuq_ref,
                    gkv_ref, wuk_ref, wuv_ref, cos_ref, sin_ref,
                    gla_ref, dq_ref, dk_ref, dv_ref, dkT_ref, dvb_ref, mq_ref, mkT_ref, mv_ref, lat_ref, kr_ref):
    tm = x_ref.shape[0]
    xn = _rms(x_ref[...], gpre_ref[...]).astype(BF16)

    yg = _dot(xn, wgla_ref[...])
    z = _dot(yg[:, 768:896].astype(BF16), wgate_ref[...]) + bgate_ref[...]
    gla_ref[:, 0:128] = yg[:, 0:128] * (GLA_DK ** -0.5)
    gla_ref[:, 128:256] = yg[:, 128:256]
    gla_ref[:, 256:384] = jax.nn.log_sigmoid(z) * (1.0 / GLA_GATE_TAU)
    gla_ref[:, 384:896] = yg[:, 256:768]

    yd = _dot(xn, wdiff_ref[...])
    dq_ref[...] = (yd[:, 0:256] * (DIFF_D ** -0.5)).astype(BF16)
    dk_ref[...] = yd[:, 256:384]
    dv_ref[...] = yd[:, 384:512]
    dvb_ref[...] = yd[:, 384:512].astype(BF16)
    dkT_ref[...] = yd[:, 512:768].T.astype(BF16)

    ym = _dot(xn, wmla_ref[...])
    cq = _rms(ym[:, 0:768], gq_ref[...]).astype(BF16)
    qf = _dot(cq, wuq_ref[...])
    lat = _rms(ym[:, 768:1024], gkv_ref[...])
    lat_ref[...] = lat
    latb = lat.astype(BF16)
    cosv = cos_ref[...]
    sinv = sin_ref[...]
    lane = lax.broadcasted_iota(jnp.int32, (tm, LANES), 1)

    def rope(x):
        partner = jnp.where(lane < MLA_ROPE // 2, pltpu.roll(x, LANES - MLA_ROPE // 2, 1),
                            pltpu.roll(x, MLA_ROPE // 2, 1))
        return x * cosv + partner * sinv

    krf = rope(ym[:, 1024:1152])
    kr_ref[...] = krf[:, 0:MLA_ROPE]
    kn = _dot(latb, wuk_ref[...])
    mv_ref[...] = _dot(latb, wuv_ref[...]).astype(BF16)
    scale = (MLA_NOPE + MLA_ROPE) ** -0.5
    for h in range(MLA_HEADS):
        sl = slice(LANES * h, LANES * (h + 1))
        mq_ref[:, sl] = (rope(qf[:, sl]) * scale).astype(BF16)
        mkT_ref[sl, :] = (kn[:, sl] + krf).T.astype(BF16)


def _in_proj(x, w, cos_t, sin_t, tm):
    t = x.shape[0]
    row = lambda n: pl.BlockSpec((tm, n), lambda i: (i, 0))
    full = lambda a: pl.BlockSpec(a.shape, lambda i: (0,) * a.ndim)
    colT = lambda n: pl.BlockSpec((n, tm), lambda i: (0, i))
    consts = [w["g_pre"], w["w_gla"], w["w_gate"], w["b_gate"], w["w_diff"], w["w_mla"], w["g_q"], w["w_uq"],
              w["g_kv"], w["w_uk"], w["w_uv"]]
    out_shape = [
        jax.ShapeDtypeStruct((t, 896), F32),
        jax.ShapeDtypeStruct((t, 256), BF16),
        jax.ShapeDtypeStruct((t, 128), F32),
        jax.ShapeDtypeStruct((t, 128), F32),
        jax.ShapeDtypeStruct((256, t), BF16),
        jax.ShapeDtypeStruct((t, 128), BF16),
        jax.ShapeDtypeStruct((t, 1024), BF16),
        jax.ShapeDtypeStruct((1024, t), BF16),
        jax.ShapeDtypeStruct((t, 512), BF16),
        jax.ShapeDtypeStruct((t, 256), F32),
        jax.ShapeDtypeStruct((t, MLA_ROPE), F32),
    ]
    out_specs = [row(896), row(256), row(128), row(128), colT(256), row(128), row(1024), colT(1024), row(512),
                 row(256), row(MLA_ROPE)]
    return pl.pallas_call(
        _in_proj_kernel,
        grid=(t // tm,),
        in_specs=[row(D_MODEL)] + [full(a) for a in consts] + [row(LANES), row(LANES)],
        out_specs=out_specs,
        out_shape=out_shape,
        compiler_params=_cparams(("arbitrary",)),
        name="in_proj",
    )(x, *consts, cos_t, sin_t)


def _gla_kernel(x_ref, *rest, chunk, long_mode, n_inner):
    if long_mode:
        o_ref, sfin_ref, st_sc, kpad_sc, vpad_sc, bpad_sc = rest
        s0_ref = None
    else:
        s0_ref, o_ref, sfin_ref, kpad_sc, vpad_sc, bpad_sc = rest
        st_sc = None
    tb = GLA_BLOCK
    nc = tb // chunk
    j = pl.program_id(1)
    q = x_ref[:, 0:128]
    k = x_ref[:, 128:256]
    la = x_ref[:, 256:384]
    v = x_ref[:, 384:640]
    vb = v.astype(BF16)

    r2 = lax.broadcasted_iota(jnp.int32, (tb, tb), 0)
    c2 = lax.broadcasted_iota(jnp.int32, (tb, tb), 1)
    la3 = _split3(la)
    lower = (c2 <= r2).astype(BF16)
    b_rows = sum(_dot(lower, p) for p in la3)
    upper = (r2 <= c2).astype(BF16)
    b_cols = sum(_dot(p, upper) for p in _split3(la.T))
    within = ((c2 <= r2) & (_div(c2, chunk) == _div(r2, chunk))).astype(BF16)
    b_chunk = sum(_dot(within, p) for p in la3)
    kT = k.T

    er = lax.broadcasted_iota(jnp.int32, (128, 256), 0)
    ec = lax.broadcasted_iota(jnp.int32, (128, 256), 1)
    same_head = _div(er, GLA_DK) == _div(ec, GLA_DV)
    expand = same_head.astype(BF16)
    head_mask = same_head.astype(F32)

    zpad = jnp.zeros((chunk, 128), F32)
    kpad_sc[0:chunk, :] = zpad
    bpad_sc[0:chunk, :] = zpad
    vpad_sc[0:chunk, :] = jnp.zeros((chunk, 256), F32)
    kpad_sc[chunk:chunk + tb, :] = k
    bpad_sc[chunk:chunk + tb, :] = b_rows
    vpad_sc[chunk:chunk + tb, :] = v
    tmod = _mod(lax.broadcasted_iota(jnp.int32, (tb, 1), 0), chunk)
    o_acc = jnp.zeros((tb, 256), F32)
    for r in range(chunk):
        k_sh = kpad_sc[chunk - r:chunk - r + tb, :]
        b_sh = bpad_sc[chunk - r:chunk - r + tb, :]
        v_sh = vpad_sc[chunk - r:chunk - r + tb, :]
        e = jnp.exp(jnp.where(tmod >= r, b_rows - b_sh, NEG_INF))
        o_acc = o_acc + _dot((q * k_sh * e).astype(BF16), expand) * v_sh

    qe = q * jnp.exp(b_chunk)
    colid = lax.broadcasted_iota(jnp.int32, (1, tb), 1)
    rowid = lax.broadcasted_iota(jnp.int32, (tb, 1), 0)
    if long_mode:
        @pl.when(j == 0)
        def _():
            st_sc[...] = jnp.zeros_like(st_sc)
        s_block = st_sc[...]
    s_use = s_block if long_mode else None
    for i in range(nc):
        end = (i + 1) * chunk - 1
        lo = 0 if long_mode else i * chunk
        bend = b_cols[:, end:end + 1]
        m = (colid >= lo) & (colid <= end)
        kp = jnp.where(m, kT * jnp.exp(jnp.where(m, bend - b_cols, 0.0)), 0.0)
        ds = _dot(kp.astype(BF16), vb) * head_mask
        if long_mode:
            s_ref_i = s_block
            decay = jnp.exp(bend)
        else:
            s_ref_i = s0_ref[i]
            s_use = s_ref_i
            decay = jnp.exp(bend - b_cols[:, lo - 1:lo]) if i > 0 else jnp.exp(bend)
        s_i = ds + decay * s_ref_i
        in_chunk = (rowid >= i * chunk) & (rowid <= end)
        o_acc = o_acc + _dot(jnp.where(in_chunk, qe, 0.0).astype(BF16), s_use.astype(BF16))
        if long_mode:
            s_use = s_i
        else:
            sfin_ref[i] = s_i
    o_ref[...] = o_acc
    if long_mode:
        st_sc[...] = s_use

        @pl.when(j == n_inner - 1)
        def _():
            sfin_ref[0] = s_use


def _gla(gla_in, row0, n_seq, seq_len, s0):
    chunk = math.gcd(seq_len, GLA_CHUNK)
    tb = GLA_BLOCK
    long_mode = s0 is None
    rows = n_seq * seq_len
    base = row0 // tb
    scratch = [pltpu.VMEM((tb + chunk, 128), F32), pltpu.VMEM((tb + chunk, 256), F32),
               pltpu.VMEM((tb + chunk, 128), F32)]
    if long_mode:
        n_inner = seq_len // tb
        grid = (n_seq, n_inner)
        in_specs = [pl.BlockSpec((tb, 896), lambda s, j: (base + s * n_inner + j, 0))]
        out_specs = [pl.BlockSpec((tb, 256), lambda s, j: (s * n_inner + j, 0)),
                     pl.BlockSpec((1, 128, 256), lambda s, j: (s, 0, 0))]
        scratch = [pltpu.VMEM((128, 256), F32)] + scratch
        args = (gla_in,)
    else:
        assert seq_len == chunk and rows % tb == 0
        n_inner = 1
        spb = tb // chunk
        grid = (rows // tb, 1)
        in_specs = [pl.BlockSpec((tb, 896), lambda s, j: (base + s, 0)),
                    pl.BlockSpec((spb, 128, 256), lambda s, j: (s, 0, 0))]
        out_specs = [pl.BlockSpec((tb, 256), lambda s, j: (s, 0)),
                     pl.BlockSpec((spb, 128, 256), lambda s, j: (s, 0, 0))]
        args = (gla_in, s0)
    return pl.pallas_call(
        functools.partial(_gla_kernel, chunk=chunk, long_mode=long_mode, n_inner=n_inner),
        grid=grid,
        in_specs=in_specs,
        out_specs=out_specs,
        out_shape=[jax.ShapeDtypeStruct((rows, 256), F32), jax.ShapeDtypeStruct((n_seq, 128, 256), F32)],
        scratch_shapes=scratch,
        compiler_params=_cparams(("arbitrary", "arbitrary")),
        name="gla_prompt" if long_mode else "gla_decode",
    )(*args)


def _diff_lambda(dl, lam_init):
    a = jnp.sum(dl[0:1, :] * dl[1:2, :], axis=1, keepdims=True)
    b = jnp.sum(dl[2:3, :] * dl[3:4, :], axis=1, keepdims=True)
    return jnp.exp(a) - jnp.exp(b) + lam_init


def _attn_kernel(qi_tab, ki_tab, q_ref, kT_ref, v_ref, *rest, mode, lam_init):
    if mode == "diff":
        lam_ref, o_ref, qh_sc, m_sc, l_sc, acc_sc = rest
    else:
        o_ref, qh_sc, m_sc, l_sc, acc_sc = rest
    tq = q_ref.shape[0]
    tk = v_ref.shape[0]
    p = pl.program_id(1)
    qi = qi_tab[p]
    ki = ki_tab[p]
    nh = 8

    @pl.when(ki == 0)
    def _():
        m_sc[...] = jnp.full(m_sc.shape, NEG_INF, F32)
        l_sc[...] = jnp.zeros_like(l_sc)
        acc_sc[...] = jnp.zeros_like(acc_sc)
        if mode == "diff":
            seg = _div(lax.broadcasted_iota(jnp.int32, (tq, LANES), 1), DIFF_D)
            for kv in range(DIFF_KV_HEADS):
                qg = q_ref[:, LANES * kv:LANES * (kv + 1)]
                for s in range(4):
                    qh_sc[kv * 4 + s] = jnp.where(seg == s, qg, jnp.zeros_like(qg))
        else:
            for h in range(nh):
                qh_sc[h] = q_ref[:, LANES * h:LANES * (h + 1)]

    row = lax.broadcasted_iota(jnp.int32, (tq, tk), 0)
    col = lax.broadcasted_iota(jnp.int32, (tq, tk), 1)
    visible = (col <= row) | (ki < qi)
    for h in range(nh):
        kg = h // 4 if mode == "diff" else h
        vg = 0 if mode == "diff" else h // 2
        s = _dot(qh_sc[h], kT_ref[LANES * kg:LANES * (kg + 1), :])
        s = jnp.where(visible, s, NEG_INF)
        m_prev = m_sc[h]
        m_new = jnp.maximum(m_prev, jnp.max(s, axis=1, keepdims=True))
        alpha = jnp.exp(m_prev - m_new)
        pe = jnp.exp(s - m_new)
        l_sc[h] = alpha * l_sc[h] + jnp.sum(pe, axis=1, keepdims=True)
        acc_sc[h] = alpha * acc_sc[h] + _dot(pe.astype(BF16), v_ref[:, LANES * vg:LANES * (vg + 1)])
        m_sc[h] = m_new

    @pl.when(ki == qi)
    def _():
        lane = lax.broadcasted_iota(jnp.int32, (tq, LANES), 1)
        if mode == "diff":
            lam = _diff_lambda(lam_ref[...], lam_init)
            for kv in range(DIFF_KV_HEADS):
                outs = []
                for g in range(2):
                    h0 = kv * 4 + g * 2
                    o = acc_sc[h0] / l_sc[h0] - lam * (acc_sc[h0 + 1] / l_sc[h0 + 1])
                    outs.append(o if kv == g else pltpu.roll(o, 64, 1))
                o_ref[:, LANES * kv:LANES * (kv + 1)] = jnp.where(lane < 64, outs[0], outs[1])
        else:
            for pr in range(nh // 2):
                o0 = acc_sc[2 * pr] / l_sc[2 * pr]
                o1 = acc_sc[2 * pr + 1] / l_sc[2 * pr + 1]
                o_ref[:, LANES * pr:LANES * (pr + 1)] = jnp.where(lane < 64, o0, o1).astype(o_ref.dtype)


def _tri_tables(n):
    qi = np.concatenate([np.full(i + 1, i, np.int32) for i in range(n)])
    ki = np.concatenate([np.arange(i + 1, dtype=np.int32) for i in range(n)])
    return jnp.asarray(qi), jnp.asarray(ki)


def _attn_prompt(q, kT, v, batch, seq, mode, lam_init=0.0, dl=None):
    tq = min(512, seq)
    nq = seq // tq
    qi_tab, ki_tab = _tri_tables(nq)
    qw = q.shape[1]
    ow, odt = (256, F32) if mode == "diff" else (512, BF16)
    in_specs = [pl.BlockSpec((tq, qw), lambda b, p, qt, kt: (b * nq + qt[p], 0)),
                pl.BlockSpec((kT.shape[0], tq), lambda b, p, qt, kt: (0, b * nq + kt[p])),
                pl.BlockSpec((tq, v.shape[1]), lambda b, p, qt, kt: (b * nq + kt[p], 0))]
    args = [q, kT, v]
    if mode == "diff":
        in_specs.append(pl.BlockSpec(dl.shape, lambda b, p, qt, kt: (0, 0)))
        args.append(dl)
    return pl.pallas_call(
        functools.partial(_attn_kernel, mode=mode, lam_init=lam_init),
        grid_spec=pltpu.PrefetchScalarGridSpec(
            num_scalar_prefetch=2,
            grid=(batch, int(qi_tab.shape[0])),
            in_specs=in_specs,
            out_specs=pl.BlockSpec((tq, ow), lambda b, p, qt, kt: (b * nq + qt[p], 0)),
            scratch_shapes=[pltpu.VMEM((8, tq, LANES), BF16), pltpu.VMEM((8, tq, 1), F32),
                            pltpu.VMEM((8, tq, 1), F32), pltpu.VMEM((8, tq, LANES), F32)],
        ),
        out_shape=jax.ShapeDtypeStruct((batch * seq, ow), odt),
        compiler_params=_cparams(("arbitrary", "arbitrary")),
        name="attn_" + mode,
    )(qi_tab, ki_tab, *args)


def _softmax_step(s, m_sc, l_sc):
    m_prev = m_sc[...]
    m_new = jnp.maximum(m_prev, jnp.max(s, axis=1, keepdims=True))
    alpha = jnp.exp(m_prev - m_new)
    pe = jnp.exp(s - m_new)
    l_sc[...] = alpha * l_sc[...] + jnp.sum(pe, axis=1, keepdims=True)
    m_sc[...] = m_new
    return alpha, pe


def _new_token_mask(n_rows, n_new):
    tok = _mod(lax.broadcasted_iota(jnp.int32, (n_rows, PAGE), 0), n_new)
    col = lax.broadcasted_iota(jnp.int32, (n_rows, PAGE), 1)
    return (col <= tok) & (col < n_new)


def _pad_rows(x, n):
    return jnp.concatenate([x, jnp.zeros((n - x.shape[0], x.shape[1]), x.dtype)], axis=0)


def _dec_diff_kernel(pt_ref, q_ref, kn_ref, vn_ref, lam_ref, *rest, n_pages, n_steps, lam_init):
    k_refs = rest[:n_pages]
    v_refs = rest[n_pages:2 * n_pages]
    o_ref, qa_sc, m_sc, l_sc, acc_sc = rest[2 * n_pages:]
    n_new = q_ref.shape[0]
    step = pl.program_id(1)
    lane8 = lax.broadcasted_iota(jnp.int32, (n_new, LANES), 1)

    @pl.when(step == 0)
    def _():
        m_sc[...] = jnp.full(m_sc.shape, NEG_INF, F32)
        l_sc[...] = jnp.zeros_like(l_sc)
        acc_sc[...] = jnp.zeros_like(acc_sc)
        for kv in range(DIFF_KV_HEADS):
            qg = q_ref[:, LANES * kv:LANES * (kv + 1)]
            for g in range(2):
                for m in range(2):
                    src = g * 2 + m
                    dst = kv * 2 + m
                    x = jnp.where(_div(lane8, DIFF_D) == src, qg, 0.0)
                    shift = ((dst - src) * DIFF_D) % LANES
                    if shift:
                        x = pltpu.roll(x, shift, 1)
                    h = kv * 4 + g * 2 + m
                    qa_sc[n_new * h:n_new * (h + 1), :] = x

    qa = qa_sc[...].astype(BF16)
    s = jnp.concatenate([_dot(qa, k_refs[j][...].astype(BF16)) for j in range(n_pages)], axis=1)
    alpha, pe = _softmax_step(s, m_sc, l_sc)
    peb = pe.astype(BF16)
    pv = sum(_dot_nt(peb[:, PAGE * j:PAGE * (j + 1)], v_refs[j][...].astype(BF16)) for j in range(n_pages))
    acc_sc[...] = alpha * acc_sc[...] + pv

    @pl.when(step == n_steps - 1)
    def _():
        s_new = _dot_nt(qa, _pad_rows(kn_ref[...], PAGE).astype(BF16))
        s_new = jnp.where(_new_token_mask(qa.shape[0], n_new), s_new, NEG_INF)
        a2, pe2 = _softmax_step(s_new, m_sc, l_sc)
        acc = a2 * acc_sc[...] + _dot(pe2.astype(BF16), _pad_rows(vn_ref[...], PAGE).astype(BF16))
        on = acc / l_sc[...]
        lam = _diff_lambda(lam_ref[...], lam_init)
        for kv in range(DIFF_KV_HEADS):
            outs = []
            for g in range(2):
                h0 = kv * 4 + g * 2
                o = on[n_new * h0:n_new * (h0 + 1)] - lam * on[n_new * (h0 + 1):n_new * (h0 + 2)]
                outs.append(o if kv == g else pltpu.roll(o, 64, 1))
            o_ref[:, LANES * kv:LANES * (kv + 1)] = jnp.where(lane8 < 64, outs[0], outs[1])


def _dec_diff(q_dec, k_new, v_new, dl, cache_k, cache_v, page_table, layer, lam_init):
    n_seq, n_pg = page_table.shape
    n_new = q_dec.shape[0] // n_seq
    npp = min(PAGES_PER_STEP, n_pg)
    n_steps = n_pg // npp
    rowb = lambda w: pl.BlockSpec((n_new, w), lambda b, s, pt: (b, 0))

    def page_spec(j):
        return pl.BlockSpec((None, None, PAGE, 128), lambda b, s, pt: (layer, pt[b, s * npp + j], 0, 0))

    pages = [page_spec(j) for j in range(npp)]
    return pl.pallas_call(
        functools.partial(_dec_diff_kernel, n_pages=npp, n_steps=n_steps, lam_init=lam_init),
        grid_spec=pltpu.PrefetchScalarGridSpec(
            num_scalar_prefetch=1,
            grid=(n_seq, n_steps),
            in_specs=[rowb(256), rowb(128), rowb(128), pl.BlockSpec(dl.shape, lambda b, s, pt: (0, 0))]
            + pages + pages,
            out_specs=rowb(256),
            scratch_shapes=[pltpu.VMEM((8 * n_new, LANES), F32), pltpu.VMEM((8 * n_new, 1), F32),
                            pltpu.VMEM((8 * n_new, 1), F32), pltpu.VMEM((8 * n_new, LANES), F32)],
        ),
        out_shape=jax.ShapeDtypeStruct((n_seq * n_new, 256), F32),
        compiler_params=_cparams(("arbitrary", "arbitrary")),
        name="dec_diff",
    )(page_table, q_dec, k_new, v_new, dl, *([cache_k] * npp), *([cache_v] * npp))


def _dec_mla_kernel(pt_ref, q_ref, latn_ref, krn_ref, wukT_ref, wuvp_ref, *rest, n_pages, n_steps):
    lat_refs = rest[:n_pages]
    kr_refs = rest[n_pages:2 * n_pages]
    o_ref, ql_sc, qr_sc, m_sc, l_sc, acc_sc = rest[2 * n_pages:]
    n_new = q_ref.shape[0]
    nh = MLA_HEADS
    n_rows = nh * n_new
    step = pl.program_id(1)
    rowhead = _div(lax.broadcasted_iota(jnp.int32, (n_rows, 1), 0), n_new)

    @pl.when(step == 0)
    def _():
        m_sc[...] = jnp.full(m_sc.shape, NEG_INF, F32)
        l_sc[...] = jnp.zeros_like(l_sc)
        acc_sc[...] = jnp.zeros_like(acc_sc)
        q8 = jnp.concatenate([q_ref[:, LANES * h:LANES * (h + 1)] for h in range(nh)], axis=0)
        qr_sc[...] = q8
        q8b = q8.astype(BF16)
        ql = jnp.zeros((n_rows, MLA_KV_RANK), F32)
        for h in range(nh):
            ql = ql + jnp.where(rowhead == h, _dot(q8b, wukT_ref[h]), 0.0)
        ql_sc[...] = ql

    qlb = ql_sc[...].astype(BF16)
    qrb = qr_sc[:, 0:MLA_ROPE].astype(BF16)
    latb = [lat_refs[j][...].astype(BF16) for j in range(n_pages)]
    s = jnp.concatenate(
        [_dot_nt(qlb, latb[j]) + _dot(qrb, kr_refs[j][...].astype(BF16)) for j in range(n_pages)], axis=1)
    alpha, pe = _softmax_step(s, m_sc, l_sc)
    peb = pe.astype(BF16)
    pv = sum(_dot(peb[:, PAGE * j:PAGE * (j + 1)], latb[j]) for j in range(n_pages))
    acc_sc[...] = alpha * acc_sc[...] + pv

    @pl.when(step == n_steps - 1)
    def _():
        latn = _pad_rows(latn_ref[...], PAGE).astype(BF16)
        s_new = _dot_nt(qlb, latn) + _dot_nt(qrb, _pad_rows(krn_ref[...], PAGE).astype(BF16))
        s_new = jnp.where(_new_token_mask(n_rows, n_new), s_new, NEG_INF)
        a2, pe2 = _softmax_step(s_new, m_sc, l_sc)
        acc = a2 * acc_sc[...] + _dot(pe2.astype(BF16), latn)
        olat = (acc / l_sc[...]).astype(BF16)
        for pr in range(nh // 2):
            h0, h1 = 2 * pr, 2 * pr + 1
            o = (_dot(olat, wuvp_ref[h0])[n_new * h0:n_new * (h0 + 1)]
                 + _dot(olat, wuvp_ref[h1])[n_new * h1:n_new * (h1 + 1)])
            o_ref[:, LANES * pr:LANES * (pr + 1)] = o.astype(o_ref.dtype)


def _dec_mla(q_dec, lat_new, kr_new, wukT, wuvp, cache_lat, cache_kr, page_table, layer):
    n_seq, n_pg = page_table.shape
    n_new = q_dec.shape[0] // n_seq
    npp = min(PAGES_PER_STEP, n_pg)
    n_steps = n_pg // npp
    rowb = lambda w: pl.BlockSpec((n_new, w), lambda b, s, pt: (b, 0))
    full = lambda a: pl.BlockSpec(a.shape, lambda b, s, pt: (0,) * a.ndim)

    def page_spec(j, shape):
        return pl.BlockSpec((None, None) + shape, lambda b, s, pt: (layer, pt[b, s * npp + j], 0, 0))

    n_rows = MLA_HEADS * n_new
    return pl.pallas_call(
        functools.partial(_dec_mla_kernel, n_pages=npp, n_steps=n_steps),
        grid_spec=pltpu.PrefetchScalarGridSpec(
            num_scalar_prefetch=1,
            grid=(n_seq, n_steps),
            in_specs=[rowb(1024), rowb(MLA_KV_RANK), rowb(MLA_ROPE), full(wukT), full(wuvp)]
            + [page_spec(j, (PAGE, MLA_KV_RANK)) for j in range(npp)]
            + [page_spec(j, (MLA_ROPE, PAGE)) for j in range(npp)],
            out_specs=rowb(512),
            scratch_shapes=[pltpu.VMEM((n_rows, MLA_KV_RANK), F32), pltpu.VMEM((n_rows, LANES), F32),
                            pltpu.VMEM((n_rows, 1), F32), pltpu.VMEM((n_rows, 1), F32),
                            pltpu.VMEM((n_rows, MLA_KV_RANK), F32)],
        ),
        out_shape=jax.ShapeDtypeStruct((n_seq * n_new, 512), BF16),
        compiler_params=_cparams(("arbitrary", "arbitrary")),
        name="dec_mla",
    )(page_table, q_dec, lat_new, kr_new, wukT, wuvp, *([cache_lat] * npp), *([cache_kr] * npp))


def _out_proj_kernel(og_ref, gate0_ref, gate1_ref, od_ref, om_ref, x_ref, wo_ref, ggla_ref, gdiff_ref, gpost_ref,
                     h_ref, *, lam_init):
    er = lax.broadcasted_iota(jnp.int32, (256, 256), 0)
    ec = lax.broadcasted_iota(jnp.int32, (256, 256), 1)
    e_avg = jnp.where(_div(er, 64) == _div(ec, 64), 1.0 / 64, 0.0).astype(BF16)
    og = og_ref[...]
    gate = jnp.concatenate([gate0_ref[...], gate1_ref[...]], axis=1)
    a = og * lax.rsqrt(_group_mean(og * og, e_avg) + EPS) * ggla_ref[...] * jax.nn.silu(gate)
    od = od_ref[...]
    d = od * lax.rsqrt(_group_mean(od * od, e_avg) + EPS) * gdiff_ref[...] * (1.0 - lam_init)
    m = (_dot(a.astype(BF16), wo_ref[0:256, :]) + _dot(d.astype(BF16), wo_ref[256:512, :])
         + _dot(om_ref[...], wo_ref[512:1024, :]))
    h_ref[...] = x_ref[...] + _rms(m, gpost_ref[...])


def _out_proj(o_gla, gla_in, o_diff, o_mla, x, w, lam_init, tm):
    t = x.shape[0]
    row = lambda n: pl.BlockSpec((tm, n), lambda i: (i, 0))
    full = lambda a: pl.BlockSpec(a.shape, lambda i: (0,) * a.ndim)
    consts = [w["w_o"], w["g_gla"], w["g_diff"], w["g_post"]]
    return pl.pallas_call(
        functools.partial(_out_proj_kernel, lam_init=lam_init),
        grid=(t // tm,),
        in_specs=[row(256), pl.BlockSpec((tm, LANES), lambda i: (i, 5)), pl.BlockSpec((tm, LANES), lambda i: (i, 6)),
                  row(256), row(512), row(D_MODEL)] + [full(a) for a in consts],
        out_specs=row(D_MODEL),
        out_shape=jax.ShapeDtypeStruct((t, D_MODEL), F32),
        compiler_params=_cparams(("arbitrary",)),
        name="out_proj",
    )(o_gla, gla_in, gla_in, o_diff, o_mla, x, *consts)


def _ffn_kernel(h_ref, gpre_ref, wa_ref, wb_ref, cw_ref, cb_ref, wd_ref, gpost_ref, *rest, decode, tiles_per_seq,
                seq_len):
    if decode:
        p1_ref, p2_ref, x_ref, a_ref, xn_sc, acc_sc, a_sc = rest
    else:
        x_ref, tail_ref, xn_sc, acc_sc, a_sc, tail_sc = rest
    tm = h_ref.shape[0]
    i = pl.program_id(0)
    j = pl.program_id(1)
    nj = pl.num_programs(1)

    @pl.when(j == 0)
    def _():
        xn_sc[...] = _rms(h_ref[...], gpre_ref[...]).astype(BF16)
        acc_sc[...] = jnp.zeros_like(acc_sc)

    xn = xn_sc[...]
    a = _dot(xn, wa_ref[...])
    b = _dot(xn, wb_ref[...])
    a_sc[8:8 + tm, :] = a
    if decode:
        a_sc[0:8, :] = jnp.zeros((8, a.shape[1]), F32)
        a_ref[...] = a
        rmod = _mod(lax.broadcasted_iota(jnp.int32, (tm, 1), 0), seq_len)
        s1 = jnp.where(rmod >= 1, a_sc[7:7 + tm, :], p1_ref[...])
        s2 = jnp.where(rmod >= 2, a_sc[6:6 + tm, :], p2_ref[...])
    else:
        @pl.when(i % tiles_per_seq == 0)
        def _():
            a_sc[0:8, :] = jnp.zeros((8, a.shape[1]), F32)

        @pl.when(i % tiles_per_seq != 0)
        def _():
            a_sc[0:8, :] = tail_sc[j]

        s1 = a_sc[7:7 + tm, :]
        s2 = a_sc[6:6 + tm, :]
        tail = a[tm - 8:tm, :]
        tail_sc[j] = tail
        tail_ref[0] = tail
    c = cb_ref[...] + s2 * cw_ref[0:1, :]
    c = c + s1 * cw_ref[1:2, :]
    c = c + a * cw_ref[2:3, :]
    y = (jax.nn.gelu(c, approximate=True) * b).astype(BF16)
    acc_sc[...] += _dot(y, wd_ref[...])

    @pl.when(j == nj - 1)
    def _():
        x_ref[...] = h_ref[...] + _rms(acc_sc[...], gpost_ref[...])


def _ffn(h, row0, rows, tm, w, seq_len, prev=None):
    decode = prev is not None
    tf = FFN_TF
    nj = D_FF // tf
    base = row0 // tm
    rowspec = lambda n: pl.BlockSpec((tm, n), lambda i, j: (base + i, 0))
    vec = lambda n: pl.BlockSpec((1, n), lambda i, j: (0, 0))
    in_specs = [rowspec(D_MODEL), vec(D_MODEL),
                pl.BlockSpec((D_MODEL, tf), lambda i, j: (0, j)),
                pl.BlockSpec((D_MODEL, tf), lambda i, j: (0, nj + j)),
                pl.BlockSpec((3, tf), lambda i, j: (0, j)),
                pl.BlockSpec((1, tf), lambda i, j: (0, j)),
                pl.BlockSpec((tf, D_MODEL), lambda i, j: (j, 0)),
                vec(D_MODEL)]
    args = [h, w["g_ffn_pre"], w["w_up"], w["w_up"], w["conv_w"], w["conv_b"], w["w_down"], w["g_ffn_post"]]
    out_x = pl.BlockSpec((tm, D_MODEL), lambda i, j: (i, 0))
    scratch = [pltpu.VMEM((tm, D_MODEL), BF16), pltpu.VMEM((tm, D_MODEL), F32), pltpu.VMEM((tm + 8, tf), F32)]
    if decode:
        assert rows == tm
        in_specs += [pl.BlockSpec((tm, tf), lambda i, j: (0, j))] * 2
        args += list(prev)
        out_specs = [out_x, pl.BlockSpec((tm, tf), lambda i, j: (0, j))]
        out_shape = [jax.ShapeDtypeStruct((rows, D_MODEL), F32), jax.ShapeDtypeStruct((rows, D_FF), F32)]
        tiles_per_seq = 1
    else:
        out_specs = [out_x, pl.BlockSpec((1, 8, tf), lambda i, j: (i, 0, j))]
        out_shape = [jax.ShapeDtypeStruct((rows, D_MODEL), F32), jax.ShapeDtypeStruct((rows // tm, 8, D_FF), F32)]
        scratch.append(pltpu.VMEM((nj, 8, tf), F32))
        tiles_per_seq = seq_len // tm
    return pl.pallas_call(
        functools.partial(_ffn_kernel, decode=decode, tiles_per_seq=tiles_per_seq, seq_len=seq_len),
        grid=(rows // tm, nj),
        in_specs=in_specs,
        out_specs=out_specs,
        out_shape=out_shape,
        scratch_shapes=scratch,
        compiler_params=_cparams(("arbitrary", "arbitrary")),
        name="ffn_decode" if decode else "ffn_prompt",
    )(*args)


_SPLITS = (128, 128, 256, 256, GLA_GATE_RANK, 256, 128, 128, MLA_Q_RANK, MLA_KV_RANK, MLA_ROPE)
_OFFS = tuple(int(v) for v in np.cumsum((0,) + _SPLITS))


def _layer_weights(l, norm_mix_pre, norm_mix_post, norm_ffn_pre, norm_ffn_post, w_in, gla_w_gate, gla_b_gate, gla_norm,
                   diff_norm, mla_q_norm, mla_w_uq, mla_kv_norm, mla_w_uk, mla_w_uv, w_o, ffn_w_up, ffn_conv_w,
                   ffn_conv_b, ffn_w_down):
    wi = w_in[l]
    part = lambda n: wi[:, _OFFS[n]:_OFFS[n + 1]]
    g_q, g_k, g_v, g_gate, g_low, d_q, d_k, d_v, c_q, c_kv, k_r = (part(n) for n in range(11))
    zc = lambda n: jnp.zeros((D_MODEL, n), F32)
    w_gla = jnp.concatenate([g_q, g_k, g_v, g_gate, g_low, zc(LANES - GLA_GATE_RANK)], axis=1)
    dk4 = d_k.reshape(D_MODEL, DIFF_KV_HEADS, 1, 2 * DIFF_D)
    d_k_rep = jnp.broadcast_to(dk4, (D_MODEL, DIFF_KV_HEADS, 2, 2 * DIFF_D)).reshape(D_MODEL, 256)
    w_diff = jnp.concatenate([d_q, d_k, d_v, d_k_rep], axis=1)
    w_mla = jnp.concatenate([c_q, c_kv, k_r, zc(LANES - MLA_ROPE)], axis=1)
    w_gate = jnp.zeros((LANES, LANES), F32).at[:GLA_GATE_RANK].set(gla_w_gate[l])
    half = MLA_ROPE // 2
    uq = mla_w_uq[l].reshape(MLA_Q_RANK, MLA_HEADS, MLA_NOPE + MLA_ROPE)
    uq = jnp.concatenate([uq[..., MLA_NOPE:MLA_NOPE + half], uq[..., MLA_NOPE + half:], uq[..., :MLA_NOPE],
                          jnp.zeros((MLA_Q_RANK, MLA_HEADS, LANES - MLA_NOPE - MLA_ROPE), F32)], axis=-1)
    uk = mla_w_uk[l]
    uk_pad = jnp.concatenate([jnp.zeros((MLA_KV_RANK, MLA_HEADS, MLA_ROPE), F32), uk,
                              jnp.zeros((MLA_KV_RANK, MLA_HEADS, LANES - MLA_NOPE - MLA_ROPE), F32)], axis=-1)
    uv = mla_w_uv[l]
    uvz = jnp.zeros_like(uv)
    even = (jnp.arange(MLA_HEADS) % 2 == 0)[None, :, None]
    uv_pad = jnp.concatenate([jnp.where(even, uv, uvz), jnp.where(even, uvz, uv)], axis=-1)
    bf = lambda a: a.astype(BF16)
    rowv = lambda a: a.reshape(1, -1)
    return {
        "g_pre": rowv(norm_mix_pre[l]), "g_post": rowv(norm_mix_post[l]),
        "g_ffn_pre": rowv(norm_ffn_pre[l]), "g_ffn_post": rowv(norm_ffn_post[l]),
        "w_gla": bf(w_gla), "w_gate": bf(w_gate), "b_gate": rowv(gla_b_gate[l]),
        "w_diff": bf(w_diff), "w_mla": bf(w_mla),
        "g_q": rowv(mla_q_norm[l]), "w_uq": bf(uq.reshape(MLA_Q_RANK, MLA_HEADS * LANES)),
        "g_kv": rowv(mla_kv_norm[l]), "w_uk": bf(uk_pad.reshape(MLA_KV_RANK, MLA_HEADS * LANES)),
        "w_uv": bf(uv.reshape(MLA_KV_RANK, MLA_HEADS * MLA_V)),
        "w_ukT": bf(jnp.transpose(uk_pad, (1, 2, 0))),
        "w_uvp": bf(jnp.transpose(uv_pad, (1, 0, 2))),
        "w_o": bf(w_o[l]),
        "g_gla": rowv(jnp.tile(gla_norm[l], GLA_HEADS)), "g_diff": rowv(jnp.tile(diff_norm[l], DIFF_HEADS)),
        "w_up": bf(ffn_w_up[l]), "conv_w": ffn_conv_w[l], "conv_b": rowv(ffn_conv_b[l]), "w_down": bf(ffn_w_down[l]),
    }


def _rope_tables(pos):
    half = MLA_ROPE // 2
    inv = ROPE_BASE ** (-jnp.arange(half, dtype=F32) / half)
    ang = pos.astype(F32)[:, None] * inv[None, :]
    cos, sin = jnp.cos(ang), jnp.sin(ang)
    n = pos.shape[0]
    cos_t = jnp.concatenate([cos, cos, jnp.ones((n, LANES - MLA_ROPE), F32)], axis=1)
    sin_t = jnp.concatenate([-sin, sin, jnp.zeros((n, LANES - MLA_ROPE), F32)], axis=1)
    return cos_t, sin_t


def _state_to_blockdiag(s):
    n = s.shape[0]
    eye = jnp.eye(GLA_HEADS, dtype=s.dtype)
    return jnp.einsum("nhdv,hg->nhdgv", s, eye).reshape(n, GLA_HEADS * GLA_DK, GLA_HEADS * GLA_DV)


def _blockdiag_to_state(s):
    n = s.shape[0]
    s5 = s.reshape(n, GLA_HEADS, GLA_DK, GLA_HEADS, GLA_DV)
    return jnp.stack([s5[:, h, :, h, :] for h in range(GLA_HEADS)], axis=1)


def kernel(x_prompt, x_sample, cache_diff_k, cache_diff_v, cache_mla_latent, cache_mla_rope, state_gla, state_ffn_conv, page_table, norm_mix_pre, norm_mix_post, norm_ffn_pre, norm_ffn_post, w_in, gla_w_gate, gla_b_gate, gla_norm, diff_lambda, diff_norm, mla_q_norm, mla_w_uq, mla_kv_norm, mla_w_uk, mla_w_uv, w_o, ffn_w_up, ffn_conv_w, ffn_conv_b, ffn_w_down):
    batch, seq, _ = x_prompt.shape
    n_dec, dec_seq, _ = x_sample.shape
    depth = w_in.shape[0]
    n_pg = page_table.shape[1]
    past_len = n_pg * PAGE
    tp = batch * seq
    td = n_dec * dec_seq
    t_all = tp + td
    tm = 512 if (t_all % 512 == 0 and tp % 512 == 0) else 128
    tm_ffn = min(1024, seq)

    x = jnp.concatenate([x_prompt.reshape(tp, D_MODEL), x_sample.reshape(td, D_MODEL)], axis=0)
    pos = jnp.concatenate([jnp.tile(jnp.arange(seq), batch), jnp.tile(past_len + jnp.arange(dec_seq), n_dec)])
    cos_t, sin_t = _rope_tables(pos)
    n_phys = cache_diff_k.shape[1]
    ck = jnp.transpose(cache_diff_k, (0, 1, 3, 4, 2)).reshape(depth, n_phys, 128, PAGE)
    cv = jnp.transpose(cache_diff_v, (0, 1, 3, 4, 2)).reshape(depth, n_phys, 128, PAGE)
    ckr = jnp.transpose(cache_mla_rope, (0, 1, 3, 2))

    outs = [[] for _ in range(12)]
    for l in range(depth):
        w = _layer_weights(l, norm_mix_pre, norm_mix_post, norm_ffn_pre, norm_ffn_post, w_in, gla_w_gate, gla_b_gate,
                           gla_norm, diff_norm, mla_q_norm, mla_w_uq, mla_kv_norm, mla_w_uk, mla_w_uv, w_o, ffn_w_up,
                           ffn_conv_w, ffn_conv_b, ffn_w_down)
        lam_init = 0.8 - 0.6 * math.exp(-0.3 * l)
        dl = diff_lambda[l].astype(F32)

        (gla_in, dq, dk, dv, dkT, dvb, mq, mkT, mv, lat, kr) = _in_proj(x, w, cos_t, sin_t, tm)

        og_p, sg_p = _gla(gla_in, 0, batch, seq, None)
        od_p = _attn_prompt(dq, dkT, dvb, batch, seq, "diff", lam_init, dl)
        om_p = _attn_prompt(mq, mkT, mv, batch, seq, "mla")
        og_s, sg_s = _gla(gla_in, tp, n_dec, dec_seq, _state_to_blockdiag(state_gla[l]))
        od_s = _dec_diff(dq[tp:].astype(F32), dk[tp:], dv[tp:], dl, ck, cv, page_table, l, lam_init)
        om_s = _dec_mla(mq[tp:].astype(F32), lat[tp:], kr[tp:], w["w_ukT"], w["w_uvp"], cache_mla_latent,
                        ckr, page_table, l)

        h = _out_proj(jnp.concatenate([og_p, og_s]), gla_in, jnp.concatenate([od_p, od_s]),
                      jnp.concatenate([om_p, om_s]), x, w, lam_init, tm)

        xp, tails = _ffn(h, 0, tp, tm_ffn, w, seq)
        st = state_ffn_conv[l]
        zrow = jnp.zeros((n_dec, dec_seq - 1, D_FF), F32)
        p1 = jnp.concatenate([st[:, 1:2], zrow], axis=1).reshape(td, D_FF)
        p2 = jnp.concatenate([st, zrow[:, 1:]], axis=1).reshape(td, D_FF)
        xs, a_dec = _ffn(h, tp, td, td, w, dec_seq, prev=(p1, p2))
        x = jnp.concatenate([xp, xs], axis=0)

        tiles_per_seq = seq // tm_ffn
        conv_p = tails.reshape(batch, tiles_per_seq, 8, D_FF)[:, -1, 6:8, :]
        conv_s = a_dec.reshape(n_dec, dec_seq, D_FF)[:, dec_seq - 2:, :]
        per_layer = (
            dk[:tp].reshape(batch, seq, DIFF_KV_HEADS, 2 * DIFF_D), dv[:tp].reshape(batch, seq, DIFF_KV_HEADS, 2 * DIFF_D),
            lat[:tp].reshape(batch, seq, MLA_KV_RANK), kr[:tp].reshape(batch, seq, MLA_ROPE),
            _blockdiag_to_state(sg_p), conv_p,
            dk[tp:].reshape(n_dec, dec_seq, DIFF_KV_HEADS, 2 * DIFF_D),
            dv[tp:].reshape(n_dec, dec_seq, DIFF_KV_HEADS, 2 * DIFF_D),
            lat[tp:].reshape(n_dec, dec_seq, MLA_KV_RANK), kr[tp:].reshape(n_dec, dec_seq, MLA_ROPE),
            _blockdiag_to_state(sg_s), conv_s,
        )
        for n, a in enumerate(per_layer):
            outs[n].append(a)

    return (x[:tp].reshape(batch, seq, D_MODEL), x[tp:].reshape(n_dec, dec_seq, D_MODEL)) + tuple(
        jnp.stack(o) for o in outs)
```

```python
import functools
import math

import jax
import jax.numpy as jnp
import numpy as np
from jax import lax
from jax.experimental import pallas as pl
from jax.experimental.pallas import tpu as pltpu

F32 = jnp.float32
BF16 = jnp.bfloat16

D_MODEL = 1024
GLA_HEADS, GLA_DK, GLA_DV = 4, 32, 64
GLA_GATE_RANK, GLA_GATE_TAU, GLA_CHUNK = 16, 16.0, 16
DIFF_HEADS, DIFF_KV_HEADS, DIFF_D = 4, 2, 32
MLA_HEADS, MLA_Q_RANK, MLA_KV_RANK, MLA_NOPE, MLA_ROPE, MLA_V = 8, 768, 256, 64, 32, 64
ROPE_BASE = 10000.0
D_FF = 2816
EPS = 1e-6
PAGE = 128

LANES = 128
VMEM_LIMIT = 56 * 1024 * 1024
NEG_INF = float("-inf")
LOG2E = math.log2(math.e)

GLA_BLOCK = 128
FFN_TF = 256


def _cparams(sem):
    return pltpu.CompilerParams(dimension_semantics=sem, vmem_limit_bytes=VMEM_LIMIT)


def _rms(x, g):
    return x * lax.rsqrt(jnp.mean(x * x, axis=-1, keepdims=True) + EPS) * g


def _dot(a, b):
    return jnp.dot(a, b, preferred_element_type=F32)


def _dot_nt(a, b):
    return lax.dot_general(a, b, (((1,), (1,)), ((), ())), preferred_element_type=F32)


def _div(x, n):
    assert n & (n - 1) == 0
    return lax.shift_right_logical(x, int(n).bit_length() - 1)


def _mod(x, n):
    assert n & (n - 1) == 0
    return x & (n - 1)


def _split3(x):
    x1 = x.astype(BF16)
    r = x - x1.astype(F32)
    x2 = r.astype(BF16)
    x3 = (r - x2.astype(F32)).astype(BF16)
    return x1, x2, x3


def _group_mean(sq, e_avg):
    hi = sq.astype(BF16)
    lo = (sq - hi.astype(F32)).astype(BF16)
    return _dot(hi, e_avg) + _dot(lo, e_avg)


def _in_proj_kernel(x_ref, gpre_ref, wgla_ref, wgate_ref, bgate_ref, wdiff_ref, wmla_ref, gq_ref, wuq_ref,
                    gkv_ref, wuk_ref, wuv_ref, cos_ref, sin_ref,
                    gla_ref, dq_ref, dk_ref, dv_ref, dkT_ref, dvb_ref, mq_ref, mkT_ref, mv_ref, lat_ref, kr_ref):
    tm = x_ref.shape[0]
    xn = _rms(x_ref[...], gpre_ref[...]).astype(BF16)

    yg = _dot(xn, wgla_ref[...])
    z = _dot(yg[:, 768:896].astype(BF16), wgate_ref[...]) + bgate_ref[...]
    gla_ref[:, 0:128] = yg[:, 0:128] * (GLA_DK ** -0.5)
    gla_ref[:, 128:256] = yg[:, 128:256]
    gla_ref[:, 256:384] = jax.nn.log_sigmoid(z) * (1.0 / GLA_GATE_TAU)
    gla_ref[:, 384:896] = yg[:, 256:768]

    yd = _dot(xn, wdiff_ref[...])
    dq_ref[...] = (yd[:, 0:256] * (DIFF_D ** -0.5 * LOG2E)).astype(BF16)
    dk_ref[...] = yd[:, 256:384]
    dv_ref[...] = yd[:, 384:512]
    dvb_ref[...] = yd[:, 384:512].astype(BF16)
    dkT_ref[...] = yd[:, 512:768].T.astype(BF16)

    ym = _dot(xn, wmla_ref[...])
    cq = _rms(ym[:, 0:768], gq_ref[...]).astype(BF16)
    qf = _dot(cq, wuq_ref[...])
    lat = _rms(ym[:, 768:1024], gkv_ref[...])
    lat_ref[...] = lat
    latb = lat.astype(BF16)
    cosv = cos_ref[...]
    sinv = sin_ref[...]
    lane = lax.broadcasted_iota(jnp.int32, (tm, LANES), 1)

    def rope(x):
        partner = jnp.where(lane < MLA_ROPE // 2, pltpu.roll(x, LANES - MLA_ROPE // 2, 1),
                            pltpu.roll(x, MLA_ROPE // 2, 1))
        return x * cosv + partner * sinv

    krf = rope(ym[:, 1024:1152])
    kr_ref[...] = krf[:, 0:MLA_ROPE]
    kn = _dot(latb, wuk_ref[...])
    mv_ref[...] = _dot(latb, wuv_ref[...]).astype(BF16)
    scale = (MLA_NOPE + MLA_ROPE) ** -0.5 * LOG2E
    for h in range(MLA_HEADS):
        sl = slice(LANES * h, LANES * (h + 1))
        mq_ref[:, sl] = (rope(qf[:, sl]) * scale).astype(BF16)
        mkT_ref[sl, :] = (kn[:, sl] + krf).T.astype(BF16)


def _in_proj(x, w, l, cos_t, sin_t, tm):
    t = x.shape[0]
    row = lambda n: pl.BlockSpec((tm, n), lambda i: (i, 0))
    full = lambda a: _layer_spec(a, l)
    colT = lambda n: pl.BlockSpec((n, tm), lambda i: (0, i))
    consts = [w["g_pre"], w["w_gla"], w["w_gate"], w["b_gate"], w["w_diff"], w["w_mla"], w["g_q"], w["w_uq"],
              w["g_kv"], w["w_uk"], w["w_uv"]]
    out_shape = [
        jax.ShapeDtypeStruct((t, 896), F32),
        jax.ShapeDtypeStruct((t, 256), BF16),
        jax.ShapeDtypeStruct((t, 128), F32),
        jax.ShapeDtypeStruct((t, 128), F32),
        jax.ShapeDtypeStruct((256, t), BF16),
        jax.ShapeDtypeStruct((t, 128), BF16),
        jax.ShapeDtypeStruct((t, 1024), BF16),
        jax.ShapeDtypeStruct((1024, t), BF16),
        jax.ShapeDtypeStruct((t, 512), BF16),
        jax.ShapeDtypeStruct((t, 256), F32),
        jax.ShapeDtypeStruct((t, MLA_ROPE), F32),
    ]
    out_specs = [row(896), row(256), row(128), row(128), colT(256), row(128), row(1024), colT(1024), row(512),
                 row(256), row(MLA_ROPE)]
    return pl.pallas_call(
        _in_proj_kernel,
        grid=(t // tm,),
        in_specs=[row(D_MODEL)] + [full(a) for a in consts] + [row(LANES), row(LANES)],
        out_specs=out_specs,
        out_shape=out_shape,
        compiler_params=_cparams(("arbitrary",)),
        name="in_proj",
    )(x, *consts, cos_t, sin_t)


def _gla_kernel(x_ref, *rest, chunk, long_mode, n_inner):
    if long_mode:
        o_ref, sfin_ref, st_sc, kpad_sc, vpad_sc, bpad_sc = rest
        s0_ref = None
    else:
        s0_ref, o_ref, sfin_ref, kpad_sc, vpad_sc, bpad_sc = rest
        st_sc = None
    tb = GLA_BLOCK
    nc = tb // chunk
    j = pl.program_id(1)
    q = x_ref[:, 0:128]
    k = x_ref[:, 128:256]
    la = x_ref[:, 256:384]
    v = x_ref[:, 384:640]
    vb = v.astype(BF16)

    r2 = lax.broadcasted_iota(jnp.int32, (tb, tb), 0)
    c2 = lax.broadcasted_iota(jnp.int32, (tb, tb), 1)
    la3 = _split3(la)
    lower = (c2 <= r2).astype(BF16)
    b_rows = sum(_dot(lower, p) for p in la3)
    upper = (r2 <= c2).astype(BF16)
    b_cols = sum(_dot(p, upper) for p in _split3(la.T))
    within = ((c2 <= r2) & (_div(c2, chunk) == _div(r2, chunk))).astype(BF16)
    b_chunk = sum(_dot(within, p) for p in la3)
    kT = k.T

    er = lax.broadcasted_iota(jnp.int32, (128, 256), 0)
    ec = lax.broadcasted_iota(jnp.int32, (128, 256), 1)
    same_head = _div(er, GLA_DK) == _div(ec, GLA_DV)
    expand = same_head.astype(BF16)
    head_mask = same_head.astype(F32)

    zpad = jnp.zeros((chunk, 128), F32)
    kpad_sc[0:chunk, :] = zpad
    bpad_sc[0:chunk, :] = zpad
    vpad_sc[0:chunk, :] = jnp.zeros((chunk, 256), F32)
    kpad_sc[chunk:chunk + tb, :] = k
    bpad_sc[chunk:chunk + tb, :] = b_rows
    vpad_sc[chunk:chunk + tb, :] = v
    tmod = _mod(lax.broadcasted_iota(jnp.int32, (tb, 1), 0), chunk)
    o_acc = jnp.zeros((tb, 256), F32)
    for r in range(chunk):
        k_sh = kpad_sc[chunk - r:chunk - r + tb, :]
        b_sh = bpad_sc[chunk - r:chunk - r + tb, :]
        v_sh = vpad_sc[chunk - r:chunk - r + tb, :]
        e = jnp.exp(jnp.where(tmod >= r, b_rows - b_sh, NEG_INF))
        o_acc = o_acc + _dot((q * k_sh * e).astype(BF16), expand) * v_sh

    qe = q * jnp.exp(b_chunk)
    colid = lax.broadcasted_iota(jnp.int32, (1, tb), 1)
    rowid = lax.broadcasted_iota(jnp.int32, (tb, 1), 0)
    if long_mode:
        @pl.when(j == 0)
        def _():
            st_sc[...] = jnp.zeros_like(st_sc)
        s_block = st_sc[...]
    s_use = s_block if long_mode else None
    for i in range(nc):
        end = (i + 1) * chunk - 1
        lo = 0 if long_mode else i * chunk
        bend = b_cols[:, end:end + 1]
        m = (colid >= lo) & (colid <= end)
        kp = jnp.where(m, kT * jnp.exp(jnp.where(m, bend - b_cols, 0.0)), 0.0)
        ds = _dot(kp.astype(BF16), vb) * head_mask
        if long_mode:
            s_ref_i = s_block
            decay = jnp.exp(bend)
        else:
            s_ref_i = s0_ref[i]
            s_use = s_ref_i
            decay = jnp.exp(bend - b_cols[:, lo - 1:lo]) if i > 0 else jnp.exp(bend)
        s_i = ds + decay * s_ref_i
        in_chunk = (rowid >= i * chunk) & (rowid <= end)
        o_acc = o_acc + _dot(jnp.where(in_chunk, qe, 0.0).astype(BF16), s_use.astype(BF16))
        if long_mode:
            s_use = s_i
        else:
            sfin_ref[i] = s_i
    o_ref[...] = o_acc
    if long_mode:
        st_sc[...] = s_use

        @pl.when(j == n_inner - 1)
        def _():
            sfin_ref[0] = s_use


def _gla(gla_in, row0, n_seq, seq_len, s0, l):
    chunk = math.gcd(seq_len, GLA_CHUNK)
    tb = GLA_BLOCK
    long_mode = s0 is None
    rows = n_seq * seq_len
    base = row0 // tb
    scratch = [pltpu.VMEM((tb + chunk, 128), F32), pltpu.VMEM((tb + chunk, 256), F32),
               pltpu.VMEM((tb + chunk, 128), F32)]
    if long_mode:
        n_inner = seq_len // tb
        grid = (n_seq, n_inner)
        in_specs = [pl.BlockSpec((tb, 896), lambda s, j: (base + s * n_inner + j, 0))]
        out_specs = [pl.BlockSpec((tb, 256), lambda s, j: (s * n_inner + j, 0)),
                     pl.BlockSpec((1, 128, 256), lambda s, j: (s, 0, 0))]
        scratch = [pltpu.VMEM((128, 256), F32)] + scratch
        args = (gla_in,)
    else:
        assert seq_len == chunk and rows % tb == 0
        n_inner = 1
        spb = tb // chunk
        grid = (rows // tb, 1)
        in_specs = [pl.BlockSpec((tb, 896), lambda s, j: (base + s, 0)),
                    pl.BlockSpec((None, spb, 128, 256), lambda s, j: (l, s, 0, 0))]
        out_specs = [pl.BlockSpec((tb, 256), lambda s, j: (s, 0)),
                     pl.BlockSpec((spb, 128, 256), lambda s, j: (s, 0, 0))]
        args = (gla_in, s0)
    return pl.pallas_call(
        functools.partial(_gla_kernel, chunk=chunk, long_mode=long_mode, n_inner=n_inner),
        grid=grid,
        in_specs=in_specs,
        out_specs=out_specs,
        out_shape=[jax.ShapeDtypeStruct((rows, 256), F32), jax.ShapeDtypeStruct((n_seq, 128, 256), F32)],
        scratch_shapes=scratch,
        compiler_params=_cparams(("arbitrary", "arbitrary")),
        name="gla_prompt" if long_mode else "gla_decode",
    )(*args)


def _diff_lambda(dl, lam_init):
    a = jnp.sum(dl[0:1, :] * dl[1:2, :], axis=1, keepdims=True)
    b = jnp.sum(dl[2:3, :] * dl[3:4, :], axis=1, keepdims=True)
    return jnp.exp(a) - jnp.exp(b) + lam_init


def _attn_kernel(qi_tab, ki_tab, q_ref, kT_ref, v_ref, *rest, mode, lam_init):
    if mode == "diff":
        lam_ref, o_ref, qh_sc, m_sc, l_sc, acc_sc = rest
    else:
        o_ref, qh_sc, m_sc, l_sc, acc_sc = rest
    tq = q_ref.shape[0]
    tk = v_ref.shape[0]
    p = pl.program_id(1)
    qi = qi_tab[p]
    ki = ki_tab[p]
    nh = 8

    @pl.when(ki == 0)
    def _():
        m_sc[...] = jnp.full(m_sc.shape, NEG_INF, F32)
        l_sc[...] = jnp.zeros_like(l_sc)
        acc_sc[...] = jnp.zeros_like(acc_sc)
        if mode == "diff":
            seg = _div(lax.broadcasted_iota(jnp.int32, (tq, LANES), 1), DIFF_D)
            for kv in range(DIFF_KV_HEADS):
                qg = q_ref[:, LANES * kv:LANES * (kv + 1)]
                for s in range(4):
                    qh_sc[kv * 4 + s] = jnp.where(seg == s, qg, jnp.zeros_like(qg))
        else:
            for h in range(nh):
                qh_sc[h] = q_ref[:, LANES * h:LANES * (h + 1)]

    nt = tk // LANES

    def step(masked):
        if masked:
            visible = (lax.broadcasted_iota(jnp.int32, (tq, tk), 1) <= lax.broadcasted_iota(jnp.int32, (tq, tk), 0))
        for h in range(nh):
            kg = h // 4 if mode == "diff" else h
            vg = 0 if mode == "diff" else h // 2
            s = _dot(qh_sc[h], kT_ref[LANES * kg:LANES * (kg + 1), :])
            if masked:
                s = jnp.where(visible, s, NEG_INF)
            tiles = [s[:, LANES * t:LANES * (t + 1)] for t in range(nt)]
            mt = functools.reduce(jnp.maximum, tiles)
            m_prev = m_sc[h]
            m_new = jnp.maximum(m_prev, jnp.max(mt, axis=1, keepdims=True))
            alpha = jnp.exp2(m_prev - m_new)
            ps = [jnp.exp2(t - m_new) for t in tiles]
            l_sc[h] = alpha * l_sc[h] + functools.reduce(jnp.add, ps)
            pb = jnp.concatenate([t.astype(BF16) for t in ps], axis=1)
            acc_sc[h] = alpha * acc_sc[h] + _dot(pb, v_ref[:, LANES * vg:LANES * (vg + 1)])
            m_sc[h] = m_new

    @pl.when(ki < qi)
    def _():
        step(False)

    @pl.when(ki == qi)
    def _():
        step(True)
        lane = lax.broadcasted_iota(jnp.int32, (tq, LANES), 1)
        norm = lambda h: acc_sc[h] / jnp.sum(l_sc[h], axis=1, keepdims=True)
        if mode == "diff":
            lam = _diff_lambda(lam_ref[...], lam_init)
            for kv in range(DIFF_KV_HEADS):
                outs = []
                for g in range(2):
                    h0 = kv * 4 + g * 2
                    o = norm(h0) - lam * norm(h0 + 1)
                    outs.append(o if kv == g else pltpu.roll(o, 64, 1))
                o_ref[:, LANES * kv:LANES * (kv + 1)] = jnp.where(lane < 64, outs[0], outs[1])
        else:
            for pr in range(nh // 2):
                o_ref[:, LANES * pr:LANES * (pr + 1)] = jnp.where(lane < 64, norm(2 * pr), norm(2 * pr + 1)).astype(
                    o_ref.dtype)


def _tri_tables(n):
    qi = np.concatenate([np.full(i + 1, i, np.int32) for i in range(n)])
    ki = np.concatenate([np.arange(i + 1, dtype=np.int32) for i in range(n)])
    return jnp.asarray(qi), jnp.asarray(ki)


def _attn_prompt(q, kT, v, batch, seq, mode, lam_init=0.0, dl=None, l=0):
    tq = min(512, seq)
    nq = seq // tq
    qi_tab, ki_tab = _tri_tables(nq)
    qw = q.shape[1]
    ow, odt = (256, F32) if mode == "diff" else (512, BF16)
    in_specs = [pl.BlockSpec((tq, qw), lambda b, p, qt, kt: (b * nq + qt[p], 0)),
                pl.BlockSpec((kT.shape[0], tq), lambda b, p, qt, kt: (0, b * nq + kt[p])),
                pl.BlockSpec((tq, v.shape[1]), lambda b, p, qt, kt: (b * nq + kt[p], 0))]
    args = [q, kT, v]
    if mode == "diff":
        in_specs.append(_layer_spec(dl, l))
        args.append(dl)
    return pl.pallas_call(
        functools.partial(_attn_kernel, mode=mode, lam_init=lam_init),
        grid_spec=pltpu.PrefetchScalarGridSpec(
            num_scalar_prefetch=2,
            grid=(batch, int(qi_tab.shape[0])),
            in_specs=in_specs,
            out_specs=pl.BlockSpec((tq, ow), lambda b, p, qt, kt: (b * nq + qt[p], 0)),
            scratch_shapes=[pltpu.VMEM((8, tq, LANES), BF16), pltpu.VMEM((8, tq, LANES), F32),
                            pltpu.VMEM((8, tq, LANES), F32), pltpu.VMEM((8, tq, LANES), F32)],
        ),
        out_shape=jax.ShapeDtypeStruct((batch * seq, ow), odt),
        compiler_params=_cparams(("arbitrary", "arbitrary")),
        name="attn_" + mode,
    )(qi_tab, ki_tab, *args)


def _softmax_tiles(tiles):
    m = jnp.max(functools.reduce(jnp.maximum, tiles), axis=1, keepdims=True)
    ps = [jnp.exp2(t - m) for t in tiles]
    l = jnp.sum(functools.reduce(jnp.add, ps), axis=1, keepdims=True)
    return [p.astype(BF16) for p in ps], l


def _new_token_mask(n_rows, n_new):
    tok = _mod(lax.broadcasted_iota(jnp.int32, (n_rows, PAGE), 0), n_new)
    col = lax.broadcasted_iota(jnp.int32, (n_rows, PAGE), 1)
    return (col <= tok) & (col < n_new)


def _pad_rows(x, n):
    return jnp.concatenate([x, jnp.zeros((n - x.shape[0], x.shape[1]), x.dtype)], axis=0)


def _dec_diff_kernel(pt_ref, q_ref, kn_ref, vn_ref, lam_ref, *rest, n_pages, lam_init):
    k_refs = rest[:n_pages]
    v_refs = rest[n_pages:2 * n_pages]
    o_ref = rest[2 * n_pages]
    n_new = q_ref.shape[0]
    lane8 = lax.broadcasted_iota(jnp.int32, (n_new, LANES), 1)

    heads = []
    for kv in range(DIFF_KV_HEADS):
        qg = q_ref[:, LANES * kv:LANES * (kv + 1)]
        for g in range(2):
            for m in range(2):
                src = g * 2 + m
                dst = kv * 2 + m
                x = jnp.where(_div(lane8, DIFF_D) == src, qg, 0.0)
                shift = ((dst - src) * DIFF_D) % LANES
                heads.append(pltpu.roll(x, shift, 1) if shift else x)
    qa = jnp.concatenate(heads, axis=0).astype(BF16)

    tiles = [_dot(qa, k_refs[j][...].astype(BF16)) for j in range(n_pages)]
    s_new = _dot_nt(qa, _pad_rows(kn_ref[...], PAGE).astype(BF16))
    tiles.append(jnp.where(_new_token_mask(qa.shape[0], n_new), s_new, NEG_INF))
    ps, l = _softmax_tiles(tiles)
    acc = _dot(ps[n_pages], _pad_rows(vn_ref[...], PAGE).astype(BF16))
    for j in range(n_pages):
        acc = acc + _dot_nt(ps[j], v_refs[j][...].astype(BF16))
    on = acc / l
    lam = _diff_lambda(lam_ref[...], lam_init)
    for kv in range(DIFF_KV_HEADS):
        outs = []
        for g in range(2):
            h0 = kv * 4 + g * 2
            o = on[n_new * h0:n_new * (h0 + 1)] - lam * on[n_new * (h0 + 1):n_new * (h0 + 2)]
            outs.append(o if kv == g else pltpu.roll(o, 64, 1))
        o_ref[:, LANES * kv:LANES * (kv + 1)] = jnp.where(lane8 < 64, outs[0], outs[1])


def _dec_diff(q_dec, k_new, v_new, dl, cache_k, cache_v, page_table, layer, lam_init):
    n_seq, n_pg = page_table.shape
    n_new = q_dec.shape[0] // n_seq
    rowb = lambda w: pl.BlockSpec((n_new, w), lambda b, pt: (b, 0))

    def page_spec(j):
        return pl.BlockSpec((None, None, PAGE, 128), lambda b, pt: (layer, pt[b, j], 0, 0))

    pages = [page_spec(j) for j in range(n_pg)]
    return pl.pallas_call(
        functools.partial(_dec_diff_kernel, n_pages=n_pg, lam_init=lam_init),
        grid_spec=pltpu.PrefetchScalarGridSpec(
            num_scalar_prefetch=1,
            grid=(n_seq,),
            in_specs=[rowb(256), rowb(128), rowb(128), _layer_spec(dl, layer)] + pages + pages,
            out_specs=rowb(256),
        ),
        out_shape=jax.ShapeDtypeStruct((n_seq * n_new, 256), F32),
        compiler_params=_cparams(("arbitrary",)),
        name="dec_diff",
    )(page_table, q_dec, k_new, v_new, dl, *([cache_k] * n_pg), *([cache_v] * n_pg))


def _dec_mla_kernel(pt_ref, q_ref, latn_ref, krn_ref, wukT_ref, wuvp_ref, *rest, n_pages):
    lat_refs = rest[:n_pages]
    kr_refs = rest[n_pages:2 * n_pages]
    o_ref = rest[2 * n_pages]
    n_new = q_ref.shape[0]
    nh = MLA_HEADS
    n_rows = nh * n_new
    rowhead = _div(lax.broadcasted_iota(jnp.int32, (n_rows, 1), 0), n_new)

    q8 = jnp.concatenate([q_ref[:, LANES * h:LANES * (h + 1)] for h in range(nh)], axis=0)
    q8b = q8.astype(BF16)
    ql = jnp.zeros((n_rows, MLA_KV_RANK), F32)
    for h in range(nh):
        ql = ql + jnp.where(rowhead == h, _dot(q8b, wukT_ref[h]), 0.0)
    qlb = ql.astype(BF16)
    qrb = q8[:, 0:MLA_ROPE].astype(BF16)

    latb = [lat_refs[j][...].astype(BF16) for j in range(n_pages)]
    tiles = [_dot_nt(qlb, latb[j]) + _dot(qrb, kr_refs[j][...].astype(BF16)) for j in range(n_pages)]
    latn = _pad_rows(latn_ref[...], PAGE).astype(BF16)
    s_new = _dot_nt(qlb, latn) + _dot_nt(qrb, _pad_rows(krn_ref[...], PAGE).astype(BF16))
    tiles.append(jnp.where(_new_token_mask(n_rows, n_new), s_new, NEG_INF))
    ps, l = _softmax_tiles(tiles)
    acc = _dot(ps[n_pages], latn)
    for j in range(n_pages):
        acc = acc + _dot(ps[j], latb[j])
    olat = (acc / l).astype(BF16)
    for pr in range(nh // 2):
        h0, h1 = 2 * pr, 2 * pr + 1
        o = (_dot(olat, wuvp_ref[h0])[n_new * h0:n_new * (h0 + 1)]
             + _dot(olat, wuvp_ref[h1])[n_new * h1:n_new * (h1 + 1)])
        o_ref[:, LANES * pr:LANES * (pr + 1)] = o.astype(o_ref.dtype)


def _dec_mla(q_dec, lat_new, kr_new, wukT, wuvp, cache_lat, cache_kr, page_table, layer):
    n_seq, n_pg = page_table.shape
    n_new = q_dec.shape[0] // n_seq
    rowb = lambda w: pl.BlockSpec((n_new, w), lambda b, pt: (b, 0))
    full = lambda a: _layer_spec(a, layer)

    def page_spec(j, shape):
        return pl.BlockSpec((None, None) + shape, lambda b, pt: (layer, pt[b, j], 0, 0))

    return pl.pallas_call(
        functools.partial(_dec_mla_kernel, n_pages=n_pg),
        grid_spec=pltpu.PrefetchScalarGridSpec(
            num_scalar_prefetch=1,
            grid=(n_seq,),
            in_specs=[rowb(1024), rowb(MLA_KV_RANK), rowb(MLA_ROPE), full(wukT), full(wuvp)]
            + [page_spec(j, (PAGE, MLA_KV_RANK)) for j in range(n_pg)]
            + [page_spec(j, (MLA_ROPE, PAGE)) for j in range(n_pg)],
            out_specs=rowb(512),
        ),
        out_shape=jax.ShapeDtypeStruct((n_seq * n_new, 512), BF16),
        compiler_params=_cparams(("arbitrary",)),
        name="dec_mla",
    )(page_table, q_dec, lat_new, kr_new, wukT, wuvp, *([cache_lat] * n_pg), *([cache_kr] * n_pg))


def _out_proj_kernel(og_ref, gate0_ref, gate1_ref, od_ref, om_ref, x_ref, wo_ref, ggla_ref, gdiff_ref, gpost_ref,
                     h_ref, *, lam_init):
    er = lax.broadcasted_iota(jnp.int32, (256, 256), 0)
    ec = lax.broadcasted_iota(jnp.int32, (256, 256), 1)
    e_avg = jnp.where(_div(er, 64) == _div(ec, 64), 1.0 / 64, 0.0).astype(BF16)
    og = og_ref[...]
    gate = jnp.concatenate([gate0_ref[...], gate1_ref[...]], axis=1)
    a = og * lax.rsqrt(_group_mean(og * og, e_avg) + EPS) * ggla_ref[...] * jax.nn.silu(gate)
    od = od_ref[...]
    d = od * lax.rsqrt(_group_mean(od * od, e_avg) + EPS) * gdiff_ref[...] * (1.0 - lam_init)
    m = (_dot(a.astype(BF16), wo_ref[0:256, :]) + _dot(d.astype(BF16), wo_ref[256:512, :])
         + _dot(om_ref[...], wo_ref[512:1024, :]))
    h_ref[...] = x_ref[...] + _rms(m, gpost_ref[...])


def _out_proj(o_gla, gla_in, o_diff, o_mla, x, w, l, lam_init, tm):
    t = x.shape[0]
    row = lambda n: pl.BlockSpec((tm, n), lambda i: (i, 0))
    full = lambda a: _layer_spec(a, l)
    consts = [w["w_o"], w["g_gla"], w["g_diff"], w["g_post"]]
    return pl.pallas_call(
        functools.partial(_out_proj_kernel, lam_init=lam_init),
        grid=(t // tm,),
        in_specs=[row(256), pl.BlockSpec((tm, LANES), lambda i: (i, 5)), pl.BlockSpec((tm, LANES), lambda i: (i, 6)),
                  row(256), row(512), row(D_MODEL)] + [full(a) for a in consts],
        out_specs=row(D_MODEL),
        out_shape=jax.ShapeDtypeStruct((t, D_MODEL), F32),
        compiler_params=_cparams(("arbitrary",)),
        name="out_proj",
    )(o_gla, gla_in, gla_in, o_diff, o_mla, x, *consts)


def _ffn_kernel(h_ref, gpre_ref, wa_ref, wb_ref, cw_ref, cb_ref, wd_ref, gpost_ref, *rest, decode, tiles_per_seq,
                seq_len):
    if decode:
        p1_ref, p2_ref, x_ref, a_ref, xn_sc, acc_sc, a_sc = rest
    else:
        x_ref, tail_ref, xn_sc, acc_sc, a_sc, tail_sc = rest
    tm = h_ref.shape[0]
    i = pl.program_id(0)
    j = pl.program_id(1)
    nj = pl.num_programs(1)

    @pl.when(j == 0)
    def _():
        xn_sc[...] = _rms(h_ref[...], gpre_ref[...]).astype(BF16)
        acc_sc[...] = jnp.zeros_like(acc_sc)

    xn = xn_sc[...]
    a = _dot(xn, wa_ref[...])
    b = _dot(xn, wb_ref[...])
    a_sc[8:8 + tm, :] = a
    if decode:
        a_sc[0:8, :] = jnp.zeros((8, a.shape[1]), F32)
        a_ref[...] = a
        rmod = _mod(lax.broadcasted_iota(jnp.int32, (tm, 1), 0), seq_len)
        s1 = jnp.where(rmod >= 1, a_sc[7:7 + tm, :], p1_ref[...])
        s2 = jnp.where(rmod >= 2, a_sc[6:6 + tm, :], p2_ref[...])
    else:
        @pl.when(i % tiles_per_seq == 0)
        def _():
            a_sc[0:8, :] = jnp.zeros((8, a.shape[1]), F32)

        @pl.when(i % tiles_per_seq != 0)
        def _():
            a_sc[0:8, :] = tail_sc[j]

        s1 = a_sc[7:7 + tm, :]
        s2 = a_sc[6:6 + tm, :]
        tail = a[tm - 8:tm, :]
        tail_sc[j] = tail
        tail_ref[0] = tail
    c = cb_ref[...] + s2 * cw_ref[0:1, :]
    c = c + s1 * cw_ref[1:2, :]
    c = c + a * cw_ref[2:3, :]
    y = (jax.nn.gelu(c, approximate=True) * b).astype(BF16)
    acc_sc[...] += _dot(y, wd_ref[...])

    @pl.when(j == nj - 1)
    def _():
        x_ref[...] = h_ref[...] + _rms(acc_sc[...], gpost_ref[...])


def _ffn(h, row0, rows, tm, w, l, seq_len, prev=None):
    decode = prev is not None
    tf = FFN_TF
    nj = D_FF // tf
    base = row0 // tm
    rowspec = lambda n: pl.BlockSpec((tm, n), lambda i, j: (base + i, 0))
    vec = lambda n: pl.BlockSpec((None, 1, n), lambda i, j: (l, 0, 0))
    in_specs = [rowspec(D_MODEL), vec(D_MODEL),
                pl.BlockSpec((None, D_MODEL, tf), lambda i, j: (l, 0, j)),
                pl.BlockSpec((None, D_MODEL, tf), lambda i, j: (l, 0, nj + j)),
                pl.BlockSpec((None, 3, tf), lambda i, j: (l, 0, j)),
                pl.BlockSpec((None, 1, tf), lambda i, j: (l, 0, j)),
                pl.BlockSpec((None, tf, D_MODEL), lambda i, j: (l, j, 0)),
                vec(D_MODEL)]
    args = [h, w["g_ffn_pre"], w["w_up"], w["w_up"], w["conv_w"], w["conv_b"], w["w_down"], w["g_ffn_post"]]
    out_x = pl.BlockSpec((tm, D_MODEL), lambda i, j: (i, 0))
    scratch = [pltpu.VMEM((tm, D_MODEL), BF16), pltpu.VMEM((tm, D_MODEL), F32), pltpu.VMEM((tm + 8, tf), F32)]
    if decode:
        assert rows == tm
        in_specs += [pl.BlockSpec((None, tm, tf), lambda i, j: (l, 0, j))] * 2
        args += list(prev)
        out_specs = [out_x, pl.BlockSpec((tm, tf), lambda i, j: (0, j))]
        out_shape = [jax.ShapeDtypeStruct((rows, D_MODEL), F32), jax.ShapeDtypeStruct((rows, D_FF), F32)]
        tiles_per_seq = 1
    else:
        out_specs = [out_x, pl.BlockSpec((1, 8, tf), lambda i, j: (i, 0, j))]
        out_shape = [jax.ShapeDtypeStruct((rows, D_MODEL), F32), jax.ShapeDtypeStruct((rows // tm, 8, D_FF), F32)]
        scratch.append(pltpu.VMEM((nj, 8, tf), F32))
        tiles_per_seq = seq_len // tm
    return pl.pallas_call(
        functools.partial(_ffn_kernel, decode=decode, tiles_per_seq=tiles_per_seq, seq_len=seq_len),
        grid=(rows // tm, nj),
        in_specs=in_specs,
        out_specs=out_specs,
        out_shape=out_shape,
        scratch_shapes=scratch,
        compiler_params=_cparams(("arbitrary", "arbitrary")),
        name="ffn_decode" if decode else "ffn_prompt",
    )(*args)


_SPLITS = (128, 128, 256, 256, GLA_GATE_RANK, 256, 128, 128, MLA_Q_RANK, MLA_KV_RANK, MLA_ROPE)
_OFFS = tuple(int(v) for v in np.cumsum((0,) + _SPLITS))


def _prep_weights(norm_mix_pre, norm_mix_post, norm_ffn_pre, norm_ffn_post, w_in, gla_w_gate, gla_b_gate, gla_norm,
                  diff_lambda, diff_norm, mla_q_norm, mla_w_uq, mla_kv_norm, mla_w_uk, mla_w_uv, w_o, ffn_w_up,
                  ffn_conv_w, ffn_conv_b, ffn_w_down):
    depth = w_in.shape[0]
    part = lambda n: w_in[:, :, _OFFS[n]:_OFFS[n + 1]]
    g_q, g_k, g_v, g_gate, g_low, d_q, d_k, d_v, c_q, c_kv, k_r = (part(n) for n in range(11))
    zc = lambda n: jnp.zeros((depth, D_MODEL, n), F32)
    w_gla = jnp.concatenate([g_q, g_k, g_v, g_gate, g_low, zc(LANES - GLA_GATE_RANK)], axis=2)
    dk4 = d_k.reshape(depth, D_MODEL, DIFF_KV_HEADS, 1, 2 * DIFF_D)
    d_k_rep = jnp.broadcast_to(dk4, (depth, D_MODEL, DIFF_KV_HEADS, 2, 2 * DIFF_D)).reshape(depth, D_MODEL, 256)
    w_diff = jnp.concatenate([d_q, d_k, d_v, d_k_rep], axis=2)
    w_mla = jnp.concatenate([c_q, c_kv, k_r, zc(LANES - MLA_ROPE)], axis=2)
    w_gate = jnp.zeros((depth, LANES, LANES), F32).at[:, :GLA_GATE_RANK].set(gla_w_gate)
    half = MLA_ROPE // 2
    uq = mla_w_uq.reshape(depth, MLA_Q_RANK, MLA_HEADS, MLA_NOPE + MLA_ROPE)
    uq = jnp.concatenate([uq[..., MLA_NOPE:MLA_NOPE + half], uq[..., MLA_NOPE + half:], uq[..., :MLA_NOPE],
                          jnp.zeros((depth, MLA_Q_RANK, MLA_HEADS, LANES - MLA_NOPE - MLA_ROPE), F32)], axis=-1)
    uk = mla_w_uk
    uk_pad = jnp.concatenate([jnp.zeros((depth, MLA_KV_RANK, MLA_HEADS, MLA_ROPE), F32), uk,
                              jnp.zeros((depth, MLA_KV_RANK, MLA_HEADS, LANES - MLA_NOPE - MLA_ROPE), F32)], axis=-1)
    uv = mla_w_uv
    uvz = jnp.zeros_like(uv)
    even = (jnp.arange(MLA_HEADS) % 2 == 0)[None, None, :, None]
    uv_pad = jnp.concatenate([jnp.where(even, uv, uvz), jnp.where(even, uvz, uv)], axis=-1)
    bf = lambda a: a.astype(BF16)
    rowv = lambda a: a.reshape(depth, 1, -1)
    return {
        "g_pre": rowv(norm_mix_pre), "g_post": rowv(norm_mix_post),
        "g_ffn_pre": rowv(norm_ffn_pre), "g_ffn_post": rowv(norm_ffn_post),
        "w_gla": bf(w_gla), "w_gate": bf(w_gate), "b_gate": rowv(gla_b_gate),
        "w_diff": bf(w_diff), "w_mla": bf(w_mla),
        "g_q": rowv(mla_q_norm), "w_uq": bf(uq.reshape(depth, MLA_Q_RANK, MLA_HEADS * LANES)),
        "g_kv": rowv(mla_kv_norm), "w_uk": bf(uk_pad.reshape(depth, MLA_KV_RANK, MLA_HEADS * LANES)),
        "w_uv": bf(uv.reshape(depth, MLA_KV_RANK, MLA_HEADS * MLA_V)),
        "w_ukT": bf(jnp.transpose(uk_pad, (0, 2, 3, 1))),
        "w_uvp": bf(jnp.transpose(uv_pad, (0, 2, 1, 3))),
        "w_o": bf(w_o), "dl": diff_lambda.astype(F32),
        "g_gla": rowv(jnp.tile(gla_norm, (1, GLA_HEADS))), "g_diff": rowv(jnp.tile(diff_norm, (1, DIFF_HEADS))),
        "w_up": bf(ffn_w_up), "conv_w": ffn_conv_w, "conv_b": rowv(ffn_conv_b), "w_down": bf(ffn_w_down),
    }


def _layer_spec(a, l):
    nd = a.ndim - 1
    return pl.BlockSpec((None,) + a.shape[1:], lambda *_: (l,) + (0,) * nd)


def _rope_tables(pos):
    half = MLA_ROPE // 2
    inv = ROPE_BASE ** (-jnp.arange(half, dtype=F32) / half)
    ang = pos.astype(F32)[:, None] * inv[None, :]
    cos, sin = jnp.cos(ang), jnp.sin(ang)
    n = pos.shape[0]
    cos_t = jnp.concatenate([cos, cos, jnp.ones((n, LANES - MLA_ROPE), F32)], axis=1)
    sin_t = jnp.concatenate([-sin, sin, jnp.zeros((n, LANES - MLA_ROPE), F32)], axis=1)
    return cos_t, sin_t


def _state_to_blockdiag(s):
    n = s.shape[0]
    eye = jnp.eye(GLA_HEADS, dtype=s.dtype)
    return jnp.einsum("nhdv,hg->nhdgv", s, eye).reshape(n, GLA_HEADS * GLA_DK, GLA_HEADS * GLA_DV)


def _blockdiag_to_state(s):
    n = s.shape[0]
    s5 = s.reshape(n, GLA_HEADS, GLA_DK, GLA_HEADS, GLA_DV)
    return jnp.stack([s5[:, h, :, h, :] for h in range(GLA_HEADS)], axis=1)


def kernel(x_prompt, x_sample, cache_diff_k, cache_diff_v, cache_mla_latent, cache_mla_rope, state_gla, state_ffn_conv, page_table, norm_mix_pre, norm_mix_post, norm_ffn_pre, norm_ffn_post, w_in, gla_w_gate, gla_b_gate, gla_norm, diff_lambda, diff_norm, mla_q_norm, mla_w_uq, mla_kv_norm, mla_w_uk, mla_w_uv, w_o, ffn_w_up, ffn_conv_w, ffn_conv_b, ffn_w_down):
    batch, seq, _ = x_prompt.shape
    n_dec, dec_seq, _ = x_sample.shape
    depth = w_in.shape[0]
    n_pg = page_table.shape[1]
    past_len = n_pg * PAGE
    tp = batch * seq
    td = n_dec * dec_seq
    t_all = tp + td
    tm = 512 if (t_all % 512 == 0 and tp % 512 == 0) else 128
    tm_ffn = min(1024, seq)

    x = jnp.concatenate([x_prompt.reshape(tp, D_MODEL), x_sample.reshape(td, D_MODEL)], axis=0)
    pos = jnp.concatenate([jnp.tile(jnp.arange(seq), batch), jnp.tile(past_len + jnp.arange(dec_seq), n_dec)])
    cos_t, sin_t = _rope_tables(pos)
    n_phys = cache_diff_k.shape[1]
    ck = jnp.transpose(cache_diff_k, (0, 1, 3, 4, 2)).reshape(depth, n_phys, 128, PAGE)
    cv = jnp.transpose(cache_diff_v, (0, 1, 3, 4, 2)).reshape(depth, n_phys, 128, PAGE)
    ckr = jnp.transpose(cache_mla_rope, (0, 1, 3, 2))

    w = _prep_weights(norm_mix_pre, norm_mix_post, norm_ffn_pre, norm_ffn_post, w_in, gla_w_gate, gla_b_gate, gla_norm,
                      diff_lambda, diff_norm, mla_q_norm, mla_w_uq, mla_kv_norm, mla_w_uk, mla_w_uv, w_o, ffn_w_up,
                      ffn_conv_w, ffn_conv_b, ffn_w_down)
    dl = w["dl"]
    s0_all = _state_to_blockdiag(state_gla.reshape((depth * n_dec,) + state_gla.shape[2:])).reshape(
        depth, n_dec, GLA_HEADS * GLA_DK, GLA_HEADS * GLA_DV)
    zrow = jnp.zeros((depth, n_dec, dec_seq - 1, D_FF), F32)
    p1_all = jnp.concatenate([state_ffn_conv[:, :, 1:2], zrow], axis=2).reshape(depth, td, D_FF)
    p2_all = jnp.concatenate([state_ffn_conv, zrow[:, :, 1:]], axis=2).reshape(depth, td, D_FF)

    outs = [[] for _ in range(12)]
    for l in range(depth):
        lam_init = 0.8 - 0.6 * math.exp(-0.3 * l)

        (gla_in, dq, dk, dv, dkT, dvb, mq, mkT, mv, lat, kr) = _in_proj(x, w, l, cos_t, sin_t, tm)

        og_p, sg_p = _gla(gla_in, 0, batch, seq, None, l)
        od_p = _attn_prompt(dq, dkT, dvb, batch, seq, "diff", lam_init, dl, l)
        om_p = _attn_prompt(mq, mkT, mv, batch, seq, "mla")
        og_s, sg_s = _gla(gla_in, tp, n_dec, dec_seq, s0_all, l)
        od_s = _dec_diff(dq[tp:].astype(F32), dk[tp:], dv[tp:], dl, ck, cv, page_table, l, lam_init)
        om_s = _dec_mla(mq[tp:].astype(F32), lat[tp:], kr[tp:], w["w_ukT"], w["w_uvp"], cache_mla_latent,
                        ckr, page_table, l)

        h = _out_proj(jnp.concatenate([og_p, og_s]), gla_in, jnp.concatenate([od_p, od_s]),
                      jnp.concatenate([om_p, om_s]), x, w, l, lam_init, tm)

        xp, tails = _ffn(h, 0, tp, tm_ffn, w, l, seq)
        xs, a_dec = _ffn(h, tp, td, td, w, l, dec_seq, prev=(p1_all, p2_all))
        x = jnp.concatenate([xp, xs], axis=0)

        tiles_per_seq = seq // tm_ffn
        conv_p = tails.reshape(batch, tiles_per_seq, 8, D_FF)[:, -1, 6:8, :]
        conv_s = a_dec.reshape(n_dec, dec_seq, D_FF)[:, dec_seq - 2:, :]
        per_layer = (
            dk[:tp].reshape(batch, seq, DIFF_KV_HEADS, 2 * DIFF_D), dv[:tp].reshape(batch, seq, DIFF_KV_HEADS, 2 * DIFF_D),
            lat[:tp].reshape(batch, seq, MLA_KV_RANK), kr[:tp].reshape(batch, seq, MLA_ROPE),
            _blockdiag_to_state(sg_p), conv_p,
            dk[tp:].reshape(n_dec, dec_seq, DIFF_KV_HEADS, 2 * DIFF_D),
            dv[tp:].reshape(n_dec, dec_seq, DIFF_KV_HEADS, 2 * DIFF_D),
            lat[tp:].reshape(n_dec, dec_seq, MLA_KV_RANK), kr[tp:].reshape(n_dec, dec_seq, MLA_ROPE),
            _blockdiag_to_state(sg_s), conv_s,
        )
        for n, a in enumerate(per_layer):
            outs[n].append(a)

    return (x[:tp].reshape(batch, seq, D_MODEL), x[tp:].reshape(n_dec, dec_seq, D_MODEL)) + tuple(
        jnp.stack(o) for o in outs)
```

```python
import functools
import math

import jax
import jax.numpy as jnp
import numpy as np
from jax import lax
from jax.experimental import pallas as pl
from jax.experimental.pallas import tpu as pltpu

F32 = jnp.float32
BF16 = jnp.bfloat16

D_MODEL = 1024
GLA_HEADS, GLA_DK, GLA_DV = 4, 32, 64
GLA_GATE_RANK, GLA_GATE_TAU, GLA_CHUNK = 16, 16.0, 16
DIFF_HEADS, DIFF_KV_HEADS, DIFF_D = 4, 2, 32
MLA_HEADS, MLA_Q_RANK, MLA_KV_RANK, MLA_NOPE, MLA_ROPE, MLA_V = 8, 768, 256, 64, 32, 64
ROPE_BASE = 10000.0
D_FF = 2816
EPS = 1e-6
PAGE = 128

LANES = 128
VMEM_LIMIT = 56 * 1024 * 1024
NEG_INF = float("-inf")
LOG2E = math.log2(math.e)

GLA_BLOCK = 128
FFN_TF = 256


def _cparams(sem):
    return pltpu.CompilerParams(dimension_semantics=sem, vmem_limit_bytes=VMEM_LIMIT)


def _rms(x, g):
    return x * lax.rsqrt(jnp.mean(x * x, axis=-1, keepdims=True) + EPS) * g


def _dot(a, b):
    return jnp.dot(a, b, preferred_element_type=F32)


def _dot_nt(a, b):
    return lax.dot_general(a, b, (((1,), (1,)), ((), ())), preferred_element_type=F32)


def _div(x, n):
    assert n & (n - 1) == 0
    return lax.shift_right_logical(x, int(n).bit_length() - 1)


def _mod(x, n):
    assert n & (n - 1) == 0
    return x & (n - 1)


def _split3(x):
    x1 = x.astype(BF16)
    r = x - x1.astype(F32)
    x2 = r.astype(BF16)
    x3 = (r - x2.astype(F32)).astype(BF16)
    return x1, x2, x3


def _group_mean(sq, e_avg):
    hi = sq.astype(BF16)
    lo = (sq - hi.astype(F32)).astype(BF16)
    return _dot(hi, e_avg) + _dot(lo, e_avg)


def _pick_rows(n_prompt_tiles, prompt_ref, decode_ref):
    return jnp.where(pl.program_id(0) < n_prompt_tiles, prompt_ref[...], decode_ref[...])


def _two_range_specs(tm, width, n_prompt_tiles):
    return [pl.BlockSpec((tm, width), lambda i: (jnp.minimum(i, n_prompt_tiles - 1), 0)),
            pl.BlockSpec((tm, width), lambda i: (jnp.maximum(i - n_prompt_tiles, 0), 0))]


def _in_proj_kernel(xp_ref, xs_ref, gpre_ref, wgla_ref, wgate_ref, bgate_ref, wdiff_ref, wmla_ref, gq_ref, wuq_ref,
                    gkv_ref, wuk_ref, wuv_ref, cos_ref, sin_ref,
                    gla_ref, dq_ref, dk_ref, dv_ref, dkT_ref, dvb_ref, mq_ref, mkT_ref, mv_ref, lat_ref, kr_ref, *,
                    n_prompt_tiles):
    tm = xp_ref.shape[0]
    xn = _rms(_pick_rows(n_prompt_tiles, xp_ref, xs_ref), gpre_ref[...]).astype(BF16)

    yg = _dot(xn, wgla_ref[...])
    z = _dot(yg[:, 768:896].astype(BF16), wgate_ref[...]) + bgate_ref[...]
    gla_ref[:, 0:128] = yg[:, 0:128] * (GLA_DK ** -0.5)
    gla_ref[:, 128:256] = yg[:, 128:256]
    gla_ref[:, 256:384] = jax.nn.log_sigmoid(z) * (1.0 / GLA_GATE_TAU)
    gla_ref[:, 384:896] = yg[:, 256:768]

    yd = _dot(xn, wdiff_ref[...])
    dq_ref[...] = (yd[:, 0:256] * (DIFF_D ** -0.5 * LOG2E)).astype(BF16)
    dk_ref[...] = yd[:, 256:384]
    dv_ref[...] = yd[:, 384:512]
    dvb_ref[...] = yd[:, 384:512].astype(BF16)
    dkT_ref[...] = yd[:, 512:768].T.astype(BF16)

    ym = _dot(xn, wmla_ref[...])
    cq = _rms(ym[:, 0:768], gq_ref[...]).astype(BF16)
    qf = _dot(cq, wuq_ref[...])
    lat = _rms(ym[:, 768:1024], gkv_ref[...])
    lat_ref[...] = lat
    latb = lat.astype(BF16)
    cosv = cos_ref[...]
    sinv = sin_ref[...]
    lane = lax.broadcasted_iota(jnp.int32, (tm, LANES), 1)

    def rope(x):
        partner = jnp.where(lane < MLA_ROPE // 2, pltpu.roll(x, LANES - MLA_ROPE // 2, 1),
                            pltpu.roll(x, MLA_ROPE // 2, 1))
        return x * cosv + partner * sinv

    krf = rope(ym[:, 1024:1152])
    kr_ref[...] = krf[:, 0:MLA_ROPE]
    kn = _dot(latb, wuk_ref[...])
    mv_ref[...] = _dot(latb, wuv_ref[...]).astype(BF16)
    scale = (MLA_NOPE + MLA_ROPE) ** -0.5 * LOG2E
    for h in range(MLA_HEADS):
        sl = slice(LANES * h, LANES * (h + 1))
        mq_ref[:, sl] = (rope(qf[:, sl]) * scale).astype(BF16)
        mkT_ref[sl, :] = (kn[:, sl] + krf).T.astype(BF16)


def _in_proj(xp, xs, w, l, cos_t, sin_t, tm):
    t = xp.shape[0] + xs.shape[0]
    n_prompt_tiles = xp.shape[0] // tm
    row = lambda n: pl.BlockSpec((tm, n), lambda i: (i, 0))
    full = lambda a: _layer_spec(a, l)
    colT = lambda n: pl.BlockSpec((n, tm), lambda i: (0, i))
    consts = [w["g_pre"], w["w_gla"], w["w_gate"], w["b_gate"], w["w_diff"], w["w_mla"], w["g_q"], w["w_uq"],
              w["g_kv"], w["w_uk"], w["w_uv"]]
    out_shape = [
        jax.ShapeDtypeStruct((t, 896), F32),
        jax.ShapeDtypeStruct((t, 256), BF16),
        jax.ShapeDtypeStruct((t, 128), F32),
        jax.ShapeDtypeStruct((t, 128), F32),
        jax.ShapeDtypeStruct((256, t), BF16),
        jax.ShapeDtypeStruct((t, 128), BF16),
        jax.ShapeDtypeStruct((t, 1024), BF16),
        jax.ShapeDtypeStruct((1024, t), BF16),
        jax.ShapeDtypeStruct((t, 512), BF16),
        jax.ShapeDtypeStruct((t, 256), F32),
        jax.ShapeDtypeStruct((t, MLA_ROPE), F32),
    ]
    out_specs = [row(896), row(256), row(128), row(128), colT(256), row(128), row(1024), colT(1024), row(512),
                 row(256), row(MLA_ROPE)]
    return pl.pallas_call(
        functools.partial(_in_proj_kernel, n_prompt_tiles=n_prompt_tiles),
        grid=(t // tm,),
        in_specs=_two_range_specs(tm, D_MODEL, n_prompt_tiles) + [full(a) for a in consts] + [row(LANES), row(LANES)],
        out_specs=out_specs,
        out_shape=out_shape,
        compiler_params=_cparams(("arbitrary",)),
        name="in_proj",
    )(xp, xs, *consts, cos_t, sin_t)


def _gla_kernel(x_ref, *rest, chunk, long_mode, n_inner):
    if long_mode:
        o_ref, sfin_ref, st_sc, kpad_sc, vpad_sc, bpad_sc = rest
        s0_ref = None
    else:
        s0_ref, o_ref, sfin_ref, kpad_sc, vpad_sc, bpad_sc = rest
        st_sc = None
    tb = GLA_BLOCK
    nc = tb // chunk
    j = pl.program_id(1)
    q = x_ref[:, 0:128]
    k = x_ref[:, 128:256]
    la = x_ref[:, 256:384]
    v = x_ref[:, 384:640]
    vb = v.astype(BF16)

    r2 = lax.broadcasted_iota(jnp.int32, (tb, tb), 0)
    c2 = lax.broadcasted_iota(jnp.int32, (tb, tb), 1)
    la3 = _split3(la)
    lower = (c2 <= r2).astype(BF16)
    b_rows = sum(_dot(lower, p) for p in la3)
    upper = (r2 <= c2).astype(BF16)
    b_cols = sum(_dot(p, upper) for p in _split3(la.T))
    within = ((c2 <= r2) & (_div(c2, chunk) == _div(r2, chunk))).astype(BF16)
    b_chunk = sum(_dot(within, p) for p in la3)
    kT = k.T

    er = lax.broadcasted_iota(jnp.int32, (128, 256), 0)
    ec = lax.broadcasted_iota(jnp.int32, (128, 256), 1)
    same_head = _div(er, GLA_DK) == _div(ec, GLA_DV)
    expand = same_head.astype(BF16)
    head_mask = same_head.astype(F32)

    zpad = jnp.zeros((chunk, 128), F32)
    kpad_sc[0:chunk, :] = zpad
    bpad_sc[0:chunk, :] = zpad
    vpad_sc[0:chunk, :] = jnp.zeros((chunk, 256), F32)
    kpad_sc[chunk:chunk + tb, :] = k
    bpad_sc[chunk:chunk + tb, :] = b_rows
    vpad_sc[chunk:chunk + tb, :] = v
    tmod = _mod(lax.broadcasted_iota(jnp.int32, (tb, 1), 0), chunk)
    o_acc = jnp.zeros((tb, 256), F32)
    for r in range(chunk):
        k_sh = kpad_sc[chunk - r:chunk - r + tb, :]
        b_sh = bpad_sc[chunk - r:chunk - r + tb, :]
        v_sh = vpad_sc[chunk - r:chunk - r + tb, :]
        e = jnp.exp(jnp.where(tmod >= r, b_rows - b_sh, NEG_INF))
        o_acc = o_acc + _dot((q * k_sh * e).astype(BF16), expand) * v_sh

    qe = q * jnp.exp(b_chunk)
    colid = lax.broadcasted_iota(jnp.int32, (1, tb), 1)
    rowid = lax.broadcasted_iota(jnp.int32, (tb, 1), 0)
    if long_mode:
        @pl.when(j == 0)
        def _():
            st_sc[...] = jnp.zeros_like(st_sc)
        s_block = st_sc[...]
    s_use = s_block if long_mode else None
    for i in range(nc):
        end = (i + 1) * chunk - 1
        lo = 0 if long_mode else i * chunk
        bend = b_cols[:, end:end + 1]
        m = (colid >= lo) & (colid <= end)
        kp = jnp.where(m, kT * jnp.exp(jnp.where(m, bend - b_cols, 0.0)), 0.0)
        ds = _dot(kp.astype(BF16), vb) * head_mask
        if long_mode:
            s_ref_i = s_block
            decay = jnp.exp(bend)
        else:
            s_ref_i = s0_ref[i]
            s_use = s_ref_i
            decay = jnp.exp(bend - b_cols[:, lo - 1:lo]) if i > 0 else jnp.exp(bend)
        s_i = ds + decay * s_ref_i
        in_chunk = (rowid >= i * chunk) & (rowid <= end)
        o_acc = o_acc + _dot(jnp.where(in_chunk, qe, 0.0).astype(BF16), s_use.astype(BF16))
        if long_mode:
            s_use = s_i
        else:
            sfin_ref[i] = s_i
    o_ref[...] = o_acc
    if long_mode:
        st_sc[...] = s_use

        @pl.when(j == n_inner - 1)
        def _():
            sfin_ref[0] = s_use


def _gla(gla_in, row0, n_seq, seq_len, s0, l):
    chunk = math.gcd(seq_len, GLA_CHUNK)
    tb = GLA_BLOCK
    long_mode = s0 is None
    rows = n_seq * seq_len
    base = row0 // tb
    scratch = [pltpu.VMEM((tb + chunk, 128), F32), pltpu.VMEM((tb + chunk, 256), F32),
               pltpu.VMEM((tb + chunk, 128), F32)]
    if long_mode:
        n_inner = seq_len // tb
        grid = (n_seq, n_inner)
        in_specs = [pl.BlockSpec((tb, 896), lambda s, j: (base + s * n_inner + j, 0))]
        out_specs = [pl.BlockSpec((tb, 256), lambda s, j: (s * n_inner + j, 0)),
                     pl.BlockSpec((1, 128, 256), lambda s, j: (s, 0, 0))]
        scratch = [pltpu.VMEM((128, 256), F32)] + scratch
        args = (gla_in,)
    else:
        assert seq_len == chunk and rows % tb == 0
        n_inner = 1
        spb = tb // chunk
        grid = (rows // tb, 1)
        in_specs = [pl.BlockSpec((tb, 896), lambda s, j: (base + s, 0)),
                    pl.BlockSpec((None, spb, 128, 256), lambda s, j: (l, s, 0, 0))]
        out_specs = [pl.BlockSpec((tb, 256), lambda s, j: (s, 0)),
                     pl.BlockSpec((spb, 128, 256), lambda s, j: (s, 0, 0))]
        args = (gla_in, s0)
    return pl.pallas_call(
        functools.partial(_gla_kernel, chunk=chunk, long_mode=long_mode, n_inner=n_inner),
        grid=grid,
        in_specs=in_specs,
        out_specs=out_specs,
        out_shape=[jax.ShapeDtypeStruct((rows, 256), F32), jax.ShapeDtypeStruct((n_seq, 128, 256), F32)],
        scratch_shapes=scratch,
        compiler_params=_cparams(("arbitrary", "arbitrary")),
        name="gla_prompt" if long_mode else "gla_decode",
    )(*args)


def _diff_lambda(dl, lam_init):
    a = jnp.sum(dl[0:1, :] * dl[1:2, :], axis=1, keepdims=True)
    b = jnp.sum(dl[2:3, :] * dl[3:4, :], axis=1, keepdims=True)
    return jnp.exp(a) - jnp.exp(b) + lam_init


def _attn_kernel(qi_tab, ki_tab, q_ref, kT_ref, v_ref, *rest, mode, lam_init):
    if mode == "diff":
        lam_ref, o_ref, qh_sc, m_sc, l_sc, acc_sc = rest
    else:
        o_ref, qh_sc, m_sc, l_sc, acc_sc = rest
    tq = q_ref.shape[0]
    tk = v_ref.shape[0]
    p = pl.program_id(1)
    qi = qi_tab[p]
    ki = ki_tab[p]
    nh = 8

    @pl.when(ki == 0)
    def _():
        m_sc[...] = jnp.full(m_sc.shape, NEG_INF, F32)
        l_sc[...] = jnp.zeros_like(l_sc)
        acc_sc[...] = jnp.zeros_like(acc_sc)
        if mode == "diff":
            seg = _div(lax.broadcasted_iota(jnp.int32, (tq, LANES), 1), DIFF_D)
            for kv in range(DIFF_KV_HEADS):
                qg = q_ref[:, LANES * kv:LANES * (kv + 1)]
                for s in range(4):
                    qh_sc[kv * 4 + s] = jnp.where(seg == s, qg, jnp.zeros_like(qg))
        else:
            for h in range(nh):
                qh_sc[h] = q_ref[:, LANES * h:LANES * (h + 1)]

    nt = tk // LANES

    def step(masked):
        if masked:
            visible = (lax.broadcasted_iota(jnp.int32, (tq, tk), 1) <= lax.broadcasted_iota(jnp.int32, (tq, tk), 0))
        for h in range(nh):
            kg = h // 4 if mode == "diff" else h
            vg = 0 if mode == "diff" else h // 2
            s = _dot(qh_sc[h], kT_ref[LANES * kg:LANES * (kg + 1), :])
            if masked:
                s = jnp.where(visible, s, NEG_INF)
            tiles = [s[:, LANES * t:LANES * (t + 1)] for t in range(nt)]
            mt = functools.reduce(jnp.maximum, tiles)
            m_prev = m_sc[h]
            m_new = jnp.maximum(m_prev, jnp.max(mt, axis=1, keepdims=True))
            alpha = jnp.exp2(m_prev - m_new)
            ps = [jnp.exp2(t - m_new) for t in tiles]
            l_sc[h] = alpha * l_sc[h] + functools.reduce(jnp.add, ps)
            pb = jnp.concatenate([t.astype(BF16) for t in ps], axis=1)
            acc_sc[h] = alpha * acc_sc[h] + _dot(pb, v_ref[:, LANES * vg:LANES * (vg + 1)])
            m_sc[h] = m_new

    @pl.when(ki < qi)
    def _():
        step(False)

    @pl.when(ki == qi)
    def _():
        step(True)
        lane = lax.broadcasted_iota(jnp.int32, (tq, LANES), 1)
        norm = lambda h: acc_sc[h] / jnp.sum(l_sc[h], axis=1, keepdims=True)
        if mode == "diff":
            lam = _diff_lambda(lam_ref[...], lam_init)
            for kv in range(DIFF_KV_HEADS):
                outs = []
                for g in range(2):
                    h0 = kv * 4 + g * 2
                    o = norm(h0) - lam * norm(h0 + 1)
                    outs.append(o if kv == g else pltpu.roll(o, 64, 1))
                o_ref[:, LANES * kv:LANES * (kv + 1)] = jnp.where(lane < 64, outs[0], outs[1])
        else:
            for pr in range(nh // 2):
                o_ref[:, LANES * pr:LANES * (pr + 1)] = jnp.where(lane < 64, norm(2 * pr), norm(2 * pr + 1)).astype(
                    o_ref.dtype)


def _tri_tables(n):
    qi = np.concatenate([np.full(i + 1, i, np.int32) for i in range(n)])
    ki = np.concatenate([np.arange(i + 1, dtype=np.int32) for i in range(n)])
    return jnp.asarray(qi), jnp.asarray(ki)


def _attn_prompt(q, kT, v, batch, seq, mode, lam_init=0.0, dl=None, l=0):
    tq = min(512, seq)
    nq = seq // tq
    qi_tab, ki_tab = _tri_tables(nq)
    qw = q.shape[1]
    ow, odt = (256, F32) if mode == "diff" else (512, BF16)
    in_specs = [pl.BlockSpec((tq, qw), lambda b, p, qt, kt: (b * nq + qt[p], 0)),
                pl.BlockSpec((kT.shape[0], tq), lambda b, p, qt, kt: (0, b * nq + kt[p])),
                pl.BlockSpec((tq, v.shape[1]), lambda b, p, qt, kt: (b * nq + kt[p], 0))]
    args = [q, kT, v]
    if mode == "diff":
        in_specs.append(_layer_spec(dl, l))
        args.append(dl)
    return pl.pallas_call(
        functools.partial(_attn_kernel, mode=mode, lam_init=lam_init),
        grid_spec=pltpu.PrefetchScalarGridSpec(
            num_scalar_prefetch=2,
            grid=(batch, int(qi_tab.shape[0])),
            in_specs=in_specs,
            out_specs=pl.BlockSpec((tq, ow), lambda b, p, qt, kt: (b * nq + qt[p], 0)),
            scratch_shapes=[pltpu.VMEM((8, tq, LANES), BF16), pltpu.VMEM((8, tq, LANES), F32),
                            pltpu.VMEM((8, tq, LANES), F32), pltpu.VMEM((8, tq, LANES), F32)],
        ),
        out_shape=jax.ShapeDtypeStruct((batch * seq, ow), odt),
        compiler_params=_cparams(("arbitrary", "arbitrary")),
        name="attn_" + mode,
    )(qi_tab, ki_tab, *args)


def _softmax_tiles(tiles):
    m = jnp.max(functools.reduce(jnp.maximum, tiles), axis=1, keepdims=True)
    ps = [jnp.exp2(t - m) for t in tiles]
    l = jnp.sum(functools.reduce(jnp.add, ps), axis=1, keepdims=True)
    return [p.astype(BF16) for p in ps], l


def _new_token_mask(n_rows, n_new):
    tok = _mod(lax.broadcasted_iota(jnp.int32, (n_rows, PAGE), 0), n_new)
    col = lax.broadcasted_iota(jnp.int32, (n_rows, PAGE), 1)
    return (col <= tok) & (col < n_new)


def _pad_rows(x, n):
    return jnp.concatenate([x, jnp.zeros((n - x.shape[0], x.shape[1]), x.dtype)], axis=0)


def _dec_diff_body(q_ref, kn_ref, vn_ref, lam_ref, k_refs, v_refs, o_ref, lam_init):
    n_pages = len(k_refs)
    n_new = q_ref.shape[0]
    lane8 = lax.broadcasted_iota(jnp.int32, (n_new, LANES), 1)

    heads = []
    for kv in range(DIFF_KV_HEADS):
        qg = q_ref[:, LANES * kv:LANES * (kv + 1)]
        for g in range(2):
            for m in range(2):
                src = g * 2 + m
                dst = kv * 2 + m
                x = jnp.where(_div(lane8, DIFF_D) == src, qg, 0.0)
                shift = ((dst - src) * DIFF_D) % LANES
                heads.append(pltpu.roll(x, shift, 1) if shift else x)
    qa = jnp.concatenate(heads, axis=0).astype(BF16)

    tiles = [_dot(qa, k_refs[j][...].astype(BF16)) for j in range(n_pages)]
    s_new = _dot_nt(qa, _pad_rows(kn_ref[...], PAGE).astype(BF16))
    tiles.append(jnp.where(_new_token_mask(qa.shape[0], n_new), s_new, NEG_INF))
    ps, l = _softmax_tiles(tiles)
    acc = _dot(ps[n_pages], _pad_rows(vn_ref[...], PAGE).astype(BF16))
    for j in range(n_pages):
        acc = acc + _dot_nt(ps[j], v_refs[j][...].astype(BF16))
    on = acc / l
    lam = _diff_lambda(lam_ref[...], lam_init)
    for kv in range(DIFF_KV_HEADS):
        outs = []
        for g in range(2):
            h0 = kv * 4 + g * 2
            o = on[n_new * h0:n_new * (h0 + 1)] - lam * on[n_new * (h0 + 1):n_new * (h0 + 2)]
            outs.append(o if kv == g else pltpu.roll(o, 64, 1))
        o_ref[:, LANES * kv:LANES * (kv + 1)] = jnp.where(lane8 < 64, outs[0], outs[1])


def _dec_mla_body(q_ref, latn_ref, krn_ref, wukT_ref, wuvp_ref, lat_refs, kr_refs, o_ref):
    n_pages = len(lat_refs)
    n_new = q_ref.shape[0]
    nh = MLA_HEADS
    n_rows = nh * n_new
    rowhead = _div(lax.broadcasted_iota(jnp.int32, (n_rows, 1), 0), n_new)

    q8 = jnp.concatenate([q_ref[:, LANES * h:LANES * (h + 1)] for h in range(nh)], axis=0)
    q8b = q8.astype(BF16)
    ql = jnp.zeros((n_rows, MLA_KV_RANK), F32)
    for h in range(nh):
        ql = ql + jnp.where(rowhead == h, _dot(q8b, wukT_ref[h]), 0.0)
    qlb = ql.astype(BF16)
    qrb = q8[:, 0:MLA_ROPE].astype(BF16)

    latb = [lat_refs[j][...].astype(BF16) for j in range(n_pages)]
    tiles = [_dot_nt(qlb, latb[j]) + _dot(qrb, kr_refs[j][...].astype(BF16)) for j in range(n_pages)]
    latn = _pad_rows(latn_ref[...], PAGE).astype(BF16)
    s_new = _dot_nt(qlb, latn) + _dot_nt(qrb, _pad_rows(krn_ref[...], PAGE).astype(BF16))
    tiles.append(jnp.where(_new_token_mask(n_rows, n_new), s_new, NEG_INF))
    ps, l = _softmax_tiles(tiles)
    acc = _dot(ps[n_pages], latn)
    for j in range(n_pages):
        acc = acc + _dot(ps[j], latb[j])
    olat = (acc / l).astype(BF16)
    for pr in range(nh // 2):
        h0, h1 = 2 * pr, 2 * pr + 1
        o = (_dot(olat, wuvp_ref[h0])[n_new * h0:n_new * (h0 + 1)]
             + _dot(olat, wuvp_ref[h1])[n_new * h1:n_new * (h1 + 1)])
        o_ref[:, LANES * pr:LANES * (pr + 1)] = o.astype(o_ref.dtype)


def _dec_attn_kernel(pt_ref, dq_ref, kn_ref, vn_ref, lam_ref, mq_ref, latn_ref, krn_ref, wukT_ref, wuvp_ref, *rest,
                     n_pages, lam_init):
    k_refs, v_refs, lat_refs, kr_refs = (rest[n_pages * n:n_pages * (n + 1)] for n in range(4))
    od_ref, om_ref = rest[4 * n_pages:]
    _dec_diff_body(dq_ref, kn_ref, vn_ref, lam_ref, k_refs, v_refs, od_ref, lam_init)
    _dec_mla_body(mq_ref, latn_ref, krn_ref, wukT_ref, wuvp_ref, lat_refs, kr_refs, om_ref)


def _dec_attn(dq_dec, k_new, v_new, dl, mq_dec, lat_new, kr_new, wukT, wuvp, cache_k, cache_v, cache_lat, cache_kr,
              page_table, layer, lam_init, row0):
    n_seq, n_pg = page_table.shape
    n_new = dq_dec.shape[0] // n_seq
    rowb = lambda w: pl.BlockSpec((n_new, w), lambda b, pt: (b, 0))
    flat = lambda w: pl.BlockSpec((n_new, w), lambda b, pt: (row0 // n_new + b, 0))
    full = lambda a: _layer_spec(a, layer)

    def pages(shape):
        return [pl.BlockSpec((None, None) + shape, functools.partial(lambda j, b, pt: (layer, pt[b, j], 0, 0), j))
                for j in range(n_pg)]

    return pl.pallas_call(
        functools.partial(_dec_attn_kernel, n_pages=n_pg, lam_init=lam_init),
        grid_spec=pltpu.PrefetchScalarGridSpec(
            num_scalar_prefetch=1,
            grid=(n_seq,),
            in_specs=[rowb(256), flat(128), flat(128), full(dl), rowb(1024), flat(MLA_KV_RANK), flat(MLA_ROPE),
                      full(wukT), full(wuvp)]
            + pages((128, PAGE)) + pages((128, PAGE)) + pages((PAGE, MLA_KV_RANK)) + pages((MLA_ROPE, PAGE)),
            out_specs=[rowb(256), rowb(512)],
        ),
        out_shape=[jax.ShapeDtypeStruct((n_seq * n_new, 256), F32), jax.ShapeDtypeStruct((n_seq * n_new, 512), BF16)],
        compiler_params=_cparams(("arbitrary",)),
        name="dec_attn",
    )(page_table, dq_dec, k_new, v_new, dl, mq_dec, lat_new, kr_new, wukT, wuvp, *([cache_k] * n_pg),
      *([cache_v] * n_pg), *([cache_lat] * n_pg), *([cache_kr] * n_pg))


def _out_proj_kernel(ogp_ref, ogs_ref, gate0_ref, gate1_ref, odp_ref, ods_ref, omp_ref, oms_ref, xp_ref, xs_ref,
                     wo_ref, ggla_ref, gdiff_ref, gpost_ref, h_ref, *, lam_init, n_prompt_tiles):
    pick = functools.partial(_pick_rows, n_prompt_tiles)
    er = lax.broadcasted_iota(jnp.int32, (256, 256), 0)
    ec = lax.broadcasted_iota(jnp.int32, (256, 256), 1)
    e_avg = jnp.where(_div(er, 64) == _div(ec, 64), 1.0 / 64, 0.0).astype(BF16)
    og = pick(ogp_ref, ogs_ref)
    gate = jnp.concatenate([gate0_ref[...], gate1_ref[...]], axis=1)
    a = og * lax.rsqrt(_group_mean(og * og, e_avg) + EPS) * ggla_ref[...] * jax.nn.silu(gate)
    od = pick(odp_ref, ods_ref)
    d = od * lax.rsqrt(_group_mean(od * od, e_avg) + EPS) * gdiff_ref[...] * (1.0 - lam_init)
    m = (_dot(a.astype(BF16), wo_ref[0:256, :]) + _dot(d.astype(BF16), wo_ref[256:512, :])
         + _dot(pick(omp_ref, oms_ref), wo_ref[512:1024, :]))
    h_ref[...] = pick(xp_ref, xs_ref) + _rms(m, gpost_ref[...])


def _out_proj(o_gla, gla_in, o_diff, o_mla, x, w, l, lam_init, tm):
    t = x[0].shape[0] + x[1].shape[0]
    n_prompt_tiles = x[0].shape[0] // tm
    row = lambda n: pl.BlockSpec((tm, n), lambda i: (i, 0))
    two = lambda width: _two_range_specs(tm, width, n_prompt_tiles)
    full = lambda a: _layer_spec(a, l)
    consts = [w["w_o"], w["g_gla"], w["g_diff"], w["g_post"]]
    return pl.pallas_call(
        functools.partial(_out_proj_kernel, lam_init=lam_init, n_prompt_tiles=n_prompt_tiles),
        grid=(t // tm,),
        in_specs=two(256) + [pl.BlockSpec((tm, LANES), lambda i: (i, 5)), pl.BlockSpec((tm, LANES), lambda i: (i, 6))]
        + two(256) + two(512) + two(D_MODEL) + [full(a) for a in consts],
        out_specs=row(D_MODEL),
        out_shape=jax.ShapeDtypeStruct((t, D_MODEL), F32),
        compiler_params=_cparams(("arbitrary",)),
        name="out_proj",
    )(*o_gla, gla_in, gla_in, *o_diff, *o_mla, *x, *consts)


def _ffn_kernel(h_ref, gpre_ref, wa_ref, wb_ref, cw_ref, cb_ref, wd_ref, gpost_ref, *rest, decode, tiles_per_seq,
                seq_len):
    if decode:
        p1_ref, p2_ref, x_ref, a_ref, xn_sc, acc_sc, a_sc = rest
    else:
        x_ref, tail_ref, xn_sc, acc_sc, a_sc, tail_sc = rest
    tm = h_ref.shape[0]
    i = pl.program_id(0)
    j = pl.program_id(1)
    nj = pl.num_programs(1)

    @pl.when(j == 0)
    def _():
        xn_sc[...] = _rms(h_ref[...], gpre_ref[...]).astype(BF16)
        acc_sc[...] = jnp.zeros_like(acc_sc)

    xn = xn_sc[...]
    a = _dot(xn, wa_ref[...])
    b = _dot(xn, wb_ref[...])
    a_sc[8:8 + tm, :] = a
    if decode:
        a_sc[0:8, :] = jnp.zeros((8, a.shape[1]), F32)
        a_ref[...] = a
        rmod = _mod(lax.broadcasted_iota(jnp.int32, (tm, 1), 0), seq_len)
        s1 = jnp.where(rmod >= 1, a_sc[7:7 + tm, :], p1_ref[...])
        s2 = jnp.where(rmod >= 2, a_sc[6:6 + tm, :], p2_ref[...])
    else:
        @pl.when(i % tiles_per_seq == 0)
        def _():
            a_sc[0:8, :] = jnp.zeros((8, a.shape[1]), F32)

        @pl.when(i % tiles_per_seq != 0)
        def _():
            a_sc[0:8, :] = tail_sc[j]

        s1 = a_sc[7:7 + tm, :]
        s2 = a_sc[6:6 + tm, :]
        tail = a[tm - 8:tm, :]
        tail_sc[j] = tail
        tail_ref[0] = tail
    c = cb_ref[...] + s2 * cw_ref[0:1, :]
    c = c + s1 * cw_ref[1:2, :]
    c = c + a * cw_ref[2:3, :]
    y = (jax.nn.gelu(c, approximate=True) * b).astype(BF16)
    acc_sc[...] += _dot(y, wd_ref[...])

    @pl.when(j == nj - 1)
    def _():
        x_ref[...] = h_ref[...] + _rms(acc_sc[...], gpost_ref[...])


def _ffn(h, row0, rows, tm, w, l, seq_len, prev=None):
    decode = prev is not None
    tf = FFN_TF
    nj = D_FF // tf
    base = row0 // tm
    rowspec = lambda n: pl.BlockSpec((tm, n), lambda i, j: (base + i, 0))
    vec = lambda n: pl.BlockSpec((None, 1, n), lambda i, j: (l, 0, 0))
    in_specs = [rowspec(D_MODEL), vec(D_MODEL),
                pl.BlockSpec((None, D_MODEL, tf), lambda i, j: (l, 0, j)),
                pl.BlockSpec((None, D_MODEL, tf), lambda i, j: (l, 0, nj + j)),
                pl.BlockSpec((None, 3, tf), lambda i, j: (l, 0, j)),
                pl.BlockSpec((None, 1, tf), lambda i, j: (l, 0, j)),
                pl.BlockSpec((None, tf, D_MODEL), lambda i, j: (l, j, 0)),
                vec(D_MODEL)]
    args = [h, w["g_ffn_pre"], w["w_up"], w["w_up"], w["conv_w"], w["conv_b"], w["w_down"], w["g_ffn_post"]]
    out_x = pl.BlockSpec((tm, D_MODEL), lambda i, j: (i, 0))
    scratch = [pltpu.VMEM((tm, D_MODEL), BF16), pltpu.VMEM((tm, D_MODEL), F32), pltpu.VMEM((tm + 8, tf), F32)]
    if decode:
        assert rows == tm
        in_specs += [pl.BlockSpec((None, tm, tf), lambda i, j: (l, 0, j))] * 2
        args += list(prev)
        out_specs = [out_x, pl.BlockSpec((tm, tf), lambda i, j: (0, j))]
        out_shape = [jax.ShapeDtypeStruct((rows, D_MODEL), F32), jax.ShapeDtypeStruct((rows, D_FF), F32)]
        tiles_per_seq = 1
    else:
        out_specs = [out_x, pl.BlockSpec((1, 8, tf), lambda i, j: (i, 0, j))]
        out_shape = [jax.ShapeDtypeStruct((rows, D_MODEL), F32), jax.ShapeDtypeStruct((rows // tm, 8, D_FF), F32)]
        scratch.append(pltpu.VMEM((nj, 8, tf), F32))
        tiles_per_seq = seq_len // tm
    return pl.pallas_call(
        functools.partial(_ffn_kernel, decode=decode, tiles_per_seq=tiles_per_seq, seq_len=seq_len),
        grid=(rows // tm, nj),
        in_specs=in_specs,
        out_specs=out_specs,
        out_shape=out_shape,
        scratch_shapes=scratch,
        compiler_params=_cparams(("arbitrary", "arbitrary")),
        name="ffn_decode" if decode else "ffn_prompt",
    )(*args)


_SPLITS = (128, 128, 256, 256, GLA_GATE_RANK, 256, 128, 128, MLA_Q_RANK, MLA_KV_RANK, MLA_ROPE)
_OFFS = tuple(int(v) for v in np.cumsum((0,) + _SPLITS))


def _prep_weights(norm_mix_pre, norm_mix_post, norm_ffn_pre, norm_ffn_post, w_in, gla_w_gate, gla_b_gate, gla_norm,
                  diff_lambda, diff_norm, mla_q_norm, mla_w_uq, mla_kv_norm, mla_w_uk, mla_w_uv, w_o, ffn_w_up,
                  ffn_conv_w, ffn_conv_b, ffn_w_down):
    depth = w_in.shape[0]
    part = lambda n: w_in[:, :, _OFFS[n]:_OFFS[n + 1]]
    g_q, g_k, g_v, g_gate, g_low, d_q, d_k, d_v, c_q, c_kv, k_r = (part(n) for n in range(11))
    zc = lambda n: jnp.zeros((depth, D_MODEL, n), F32)
    w_gla = jnp.concatenate([g_q, g_k, g_v, g_gate, g_low, zc(LANES - GLA_GATE_RANK)], axis=2)
    dk4 = d_k.reshape(depth, D_MODEL, DIFF_KV_HEADS, 1, 2 * DIFF_D)
    d_k_rep = jnp.broadcast_to(dk4, (depth, D_MODEL, DIFF_KV_HEADS, 2, 2 * DIFF_D)).reshape(depth, D_MODEL, 256)
    w_diff = jnp.concatenate([d_q, d_k, d_v, d_k_rep], axis=2)
    w_mla = jnp.concatenate([c_q, c_kv, k_r, zc(LANES - MLA_ROPE)], axis=2)
    w_gate = jnp.zeros((depth, LANES, LANES), F32).at[:, :GLA_GATE_RANK].set(gla_w_gate)
    half = MLA_ROPE // 2
    uq = mla_w_uq.reshape(depth, MLA_Q_RANK, MLA_HEADS, MLA_NOPE + MLA_ROPE)
    uq = jnp.concatenate([uq[..., MLA_NOPE:MLA_NOPE + half], uq[..., MLA_NOPE + half:], uq[..., :MLA_NOPE],
                          jnp.zeros((depth, MLA_Q_RANK, MLA_HEADS, LANES - MLA_NOPE - MLA_ROPE), F32)], axis=-1)
    uk = mla_w_uk
    uk_pad = jnp.concatenate([jnp.zeros((depth, MLA_KV_RANK, MLA_HEADS, MLA_ROPE), F32), uk,
                              jnp.zeros((depth, MLA_KV_RANK, MLA_HEADS, LANES - MLA_NOPE - MLA_ROPE), F32)], axis=-1)
    uv = mla_w_uv
    uvz = jnp.zeros_like(uv)
    even = (jnp.arange(MLA_HEADS) % 2 == 0)[None, None, :, None]
    uv_pad = jnp.concatenate([jnp.where(even, uv, uvz), jnp.where(even, uvz, uv)], axis=-1)
    bf = lambda a: a.astype(BF16)
    rowv = lambda a: a.reshape(depth, 1, -1)
    return {
        "g_pre": rowv(norm_mix_pre), "g_post": rowv(norm_mix_post),
        "g_ffn_pre": rowv(norm_ffn_pre), "g_ffn_post": rowv(norm_ffn_post),
        "w_gla": bf(w_gla), "w_gate": bf(w_gate), "b_gate": rowv(gla_b_gate),
        "w_diff": bf(w_diff), "w_mla": bf(w_mla),
        "g_q": rowv(mla_q_norm), "w_uq": bf(uq.reshape(depth, MLA_Q_RANK, MLA_HEADS * LANES)),
        "g_kv": rowv(mla_kv_norm), "w_uk": bf(uk_pad.reshape(depth, MLA_KV_RANK, MLA_HEADS * LANES)),
        "w_uv": bf(uv.reshape(depth, MLA_KV_RANK, MLA_HEADS * MLA_V)),
        "w_ukT": bf(jnp.transpose(uk_pad, (0, 2, 3, 1))),
        "w_uvp": bf(jnp.transpose(uv_pad, (0, 2, 1, 3))),
        "w_o": bf(w_o), "dl": diff_lambda.astype(F32),
        "g_gla": rowv(jnp.tile(gla_norm, (1, GLA_HEADS))), "g_diff": rowv(jnp.tile(diff_norm, (1, DIFF_HEADS))),
        "w_up": bf(ffn_w_up), "conv_w": ffn_conv_w, "conv_b": rowv(ffn_conv_b), "w_down": bf(ffn_w_down),
    }


def _layer_spec(a, l):
    nd = a.ndim - 1
    return pl.BlockSpec((None,) + a.shape[1:], lambda *_: (l,) + (0,) * nd)


def _rope_tables(pos):
    half = MLA_ROPE // 2
    inv = ROPE_BASE ** (-jnp.arange(half, dtype=F32) / half)
    ang = pos.astype(F32)[:, None] * inv[None, :]
    cos, sin = jnp.cos(ang), jnp.sin(ang)
    n = pos.shape[0]
    cos_t = jnp.concatenate([cos, cos, jnp.ones((n, LANES - MLA_ROPE), F32)], axis=1)
    sin_t = jnp.concatenate([-sin, sin, jnp.zeros((n, LANES - MLA_ROPE), F32)], axis=1)
    return cos_t, sin_t


def _state_to_blockdiag(s):
    n = s.shape[0]
    eye = jnp.eye(GLA_HEADS, dtype=s.dtype)
    return jnp.einsum("nhdv,hg->nhdgv", s, eye).reshape(n, GLA_HEADS * GLA_DK, GLA_HEADS * GLA_DV)


def _blockdiag_to_state(s):
    n = s.shape[0]
    s5 = s.reshape(n, GLA_HEADS, GLA_DK, GLA_HEADS, GLA_DV)
    return jnp.stack([s5[:, h, :, h, :] for h in range(GLA_HEADS)], axis=1)


def kernel(x_prompt, x_sample, cache_diff_k, cache_diff_v, cache_mla_latent, cache_mla_rope, state_gla, state_ffn_conv, page_table, norm_mix_pre, norm_mix_post, norm_ffn_pre, norm_ffn_post, w_in, gla_w_gate, gla_b_gate, gla_norm, diff_lambda, diff_norm, mla_q_norm, mla_w_uq, mla_kv_norm, mla_w_uk, mla_w_uv, w_o, ffn_w_up, ffn_conv_w, ffn_conv_b, ffn_w_down):
    batch, seq, _ = x_prompt.shape
    n_dec, dec_seq, _ = x_sample.shape
    depth = w_in.shape[0]
    n_pg = page_table.shape[1]
    past_len = n_pg * PAGE
    tp = batch * seq
    td = n_dec * dec_seq
    tm = 512 if (tp % 512 == 0 and td % 512 == 0) else 128
    assert tp % tm == 0 and td % tm == 0
    tm_ffn = min(1024, seq)

    xp = x_prompt.reshape(tp, D_MODEL)
    xs = x_sample.reshape(td, D_MODEL)
    pos = jnp.concatenate([jnp.tile(jnp.arange(seq), batch), jnp.tile(past_len + jnp.arange(dec_seq), n_dec)])
    cos_t, sin_t = _rope_tables(pos)
    n_phys = cache_diff_k.shape[1]
    ck = jnp.transpose(cache_diff_k, (0, 1, 3, 4, 2)).reshape(depth, n_phys, 128, PAGE)
    cv = jnp.transpose(cache_diff_v, (0, 1, 3, 4, 2)).reshape(depth, n_phys, 128, PAGE)
    ckr = jnp.transpose(cache_mla_rope, (0, 1, 3, 2))

    w = _prep_weights(norm_mix_pre, norm_mix_post, norm_ffn_pre, norm_ffn_post, w_in, gla_w_gate, gla_b_gate, gla_norm,
                      diff_lambda, diff_norm, mla_q_norm, mla_w_uq, mla_kv_norm, mla_w_uk, mla_w_uv, w_o, ffn_w_up,
                      ffn_conv_w, ffn_conv_b, ffn_w_down)
    dl = w["dl"]
    s0_all = _state_to_blockdiag(state_gla.reshape((depth * n_dec,) + state_gla.shape[2:])).reshape(
        depth, n_dec, GLA_HEADS * GLA_DK, GLA_HEADS * GLA_DV)
    zrow = jnp.zeros((depth, n_dec, dec_seq - 1, D_FF), F32)
    p1_all = jnp.concatenate([state_ffn_conv[:, :, 1:2], zrow], axis=2).reshape(depth, td, D_FF)
    p2_all = jnp.concatenate([state_ffn_conv, zrow[:, :, 1:]], axis=2).reshape(depth, td, D_FF)

    outs = [[] for _ in range(12)]
    for l in range(depth):
        lam_init = 0.8 - 0.6 * math.exp(-0.3 * l)

        (gla_in, dq, dk, dv, dkT, dvb, mq, mkT, mv, lat, kr) = _in_proj(xp, xs, w, l, cos_t, sin_t, tm)

        og_p, sg_p = _gla(gla_in, 0, batch, seq, None, l)
        od_p = _attn_prompt(dq, dkT, dvb, batch, seq, "diff", lam_init, dl, l)
        om_p = _attn_prompt(mq, mkT, mv, batch, seq, "mla")
        og_s, sg_s = _gla(gla_in, tp, n_dec, dec_seq, s0_all, l)
        od_s, om_s = _dec_attn(dq[tp:].astype(F32), dk, dv, dl, mq[tp:].astype(F32), lat, kr, w["w_ukT"], w["w_uvp"],
                               ck, cv, cache_mla_latent, ckr, page_table, l, lam_init, tp)

        h = _out_proj((og_p, og_s), gla_in, (od_p, od_s), (om_p, om_s), (xp, xs), w, l, lam_init, tm)

        xp, tails = _ffn(h, 0, tp, tm_ffn, w, l, seq)
        xs, a_dec = _ffn(h, tp, td, td, w, l, dec_seq, prev=(p1_all, p2_all))

        tiles_per_seq = seq // tm_ffn
        conv_p = tails.reshape(batch, tiles_per_seq, 8, D_FF)[:, -1, 6:8, :]
        conv_s = a_dec.reshape(n_dec, dec_seq, D_FF)[:, dec_seq - 2:, :]
        per_layer = (
            dk[:tp].reshape(batch, seq, DIFF_KV_HEADS, 2 * DIFF_D), dv[:tp].reshape(batch, seq, DIFF_KV_HEADS, 2 * DIFF_D),
            lat[:tp].reshape(batch, seq, MLA_KV_RANK), kr[:tp].reshape(batch, seq, MLA_ROPE),
            _blockdiag_to_state(sg_p), conv_p,
            dk[tp:].reshape(n_dec, dec_seq, DIFF_KV_HEADS, 2 * DIFF_D),
            dv[tp:].reshape(n_dec, dec_seq, DIFF_KV_HEADS, 2 * DIFF_D),
            lat[tp:].reshape(n_dec, dec_seq, MLA_KV_RANK), kr[tp:].reshape(n_dec, dec_seq, MLA_ROPE),
            _blockdiag_to_state(sg_s), conv_s,
        )
        for n, a in enumerate(per_layer):
            outs[n].append(a)

    return (xp.reshape(batch, seq, D_MODEL), xs.reshape(n_dec, dec_seq, D_MODEL)) + tuple(jnp.stack(o) for o in outs)
```
